```python
import jax, jax.numpy as jnp
from jax import lax
import numpy as np

D_MODEL = 1024
BATCH = 8
SEQ = 4096
DEPTH = 2

CHUNK = 64
SB_HEADS = 8
SB_HEAD_DIM = 64
SB_WIDTH = SB_HEADS * SB_HEAD_DIM
SB_BLOCK = 128
SG_GROUPS = 4
SG_GROUP_DIM = 128
SG_WIDTH = SG_GROUPS * SG_GROUP_DIM
SG_CHUNK = 128
POOL_WINDOWS = (2, 4, 8, 16)
POOL_GROUPS = len(POOL_WINDOWS)
POOL_GROUP_DIM = 128
POOL_WIDTH = POOL_GROUPS * POOL_GROUP_DIM
N_BRANCHES = 3
BRANCH_WIDTH = 512
D_FF = 2816
CONV_WIDTH = 3
EPS = 1e-6
IN_SIZES = (SB_WIDTH, SB_WIDTH, SB_WIDTH, SG_WIDTH, SG_WIDTH, POOL_WIDTH, N_BRANCHES * D_MODEL)
IN_WIDTH = sum(IN_SIZES)
IN_SPLITS = tuple(int(i) for i in np.cumsum(IN_SIZES)[:-1])

kernel_name = "hybrid_stickbreak_sgu_pool_block"


def rmsnorm(x, g):
    xf = x.astype(jnp.float32)
    y = xf * lax.rsqrt(jnp.mean(xf * xf, axis=-1, keepdims=True) + EPS)
    return (y * g.astype(jnp.float32)).astype(x.dtype)


def layernorm(x, g, b):
    xf = x.astype(jnp.float32)
    mu = jnp.mean(xf, axis=-1, keepdims=True)
    var = jnp.mean(jnp.square(xf - mu), axis=-1, keepdims=True)
    y = (xf - mu) * lax.rsqrt(var + EPS)
    return (y * g.astype(jnp.float32) + b.astype(jnp.float32)).astype(x.dtype)


def stick_breaking_attention(q, k, v):
    _, _, S, Dh = q.shape
    scale = Dh ** -0.5
    outs = []
    for blk in range(S // SB_BLOCK):
        q0 = blk * SB_BLOCK
        end = q0 + SB_BLOCK
        qb = q[:, :, q0:end].astype(jnp.float32)
        kb = k[:, :, :end].astype(jnp.float32)
        vb = v[:, :, :end]
        z = jnp.einsum('bhtd,bhsd->bhts', qb, kb) * scale
        t_idx = q0 + jnp.arange(SB_BLOCK)[:, None]
        s_idx = jnp.arange(end)[None, :]
        strict = s_idx < t_idx
        log_keep = jnp.where(strict, jax.nn.log_sigmoid(-z), 0.0)
        after = lax.cumsum(log_keep, axis=3, reverse=True) - log_keep
        w = jnp.where(strict, jnp.exp(jax.nn.log_sigmoid(z) + after), 0.0)
        outs.append(jnp.einsum('bhts,bhsd->bhtd', w.astype(vb.dtype), vb))
    return jnp.concatenate(outs, axis=2)


def spatial_gating(u, v, w_s, b_s, ln_g, ln_b):
    B, S, _ = u.shape
    u = jax.nn.gelu(u)
    v = layernorm(jax.nn.gelu(v), ln_g, ln_b)
    vc = v.reshape(B, S // SG_CHUNK, SG_CHUNK, SG_GROUPS, SG_GROUP_DIM)
    causal = jnp.tril(jnp.ones((SG_CHUNK, SG_CHUNK), dtype=bool))
    w = jnp.where(causal[None], w_s, 0.0).astype(vc.dtype)
    mixed = jnp.einsum('gts,bnsgc->bntgc', w, vc) + b_s.T[None, None, :, :, None]
    return u * mixed.reshape(B, S, SG_WIDTH)


def multiscale_pool(xp, w_pool, scale):
    B, S, _ = xp.shape
    xg = xp.astype(jnp.float32).reshape(B, S, POOL_GROUPS, POOL_GROUP_DIM)
    cs = jnp.cumsum(xg, axis=1)
    n_valid = jnp.arange(1, S + 1)
    outs = []
    for g, win in enumerate(POOL_WINDOWS):
        c = cs[:, :, g]
        lagged = jnp.pad(c, ((0, 0), (win, 0), (0, 0)))[:, :S]
        count = jnp.minimum(n_valid, win).astype(jnp.float32)[None, :, None]
        pooled = (c - lagged) / count - xg[:, :, g]
        outs.append(pooled @ w_pool[g].astype(jnp.float32))
    y = jnp.concatenate(outs, axis=-1) * scale.astype(jnp.float32)
    return y.astype(xp.dtype)


def causal_depthwise_conv(h, w, b):
    S = h.shape[1]
    hp = jnp.pad(h, ((0, 0), (CONV_WIDTH - 1, 0), (0, 0)))
    y = b
    for i in range(CONV_WIDTH):
        y = y + hp[:, i:i + S] * w[i]
    return y


def setup_inputs(seed: int = 0) -> dict:
    key = jax.random.key(seed)
    ks = jax.random.split(key, 24)
    f32 = jnp.float32
    nrm = lambda k, shape, s: jax.random.normal(k, shape, f32) * s
    gain = lambda k, shape: 1.0 + 0.05 * jax.random.normal(k, shape, f32)
    L = DEPTH
    return {
        "x": jax.random.normal(ks[0], (BATCH, SEQ, D_MODEL), f32),
        "norm_pre_mix": gain(ks[1], (L, D_MODEL)),
        "w_in": nrm(ks[2], (L, D_MODEL, IN_WIDTH), D_MODEL ** -0.5),
        "sg_ln_g": gain(ks[3], (L, SG_WIDTH)),
        "sg_ln_b": nrm(ks[4], (L, SG_WIDTH), 0.02),
        "sg_w": nrm(ks[5], (L, SG_GROUPS, SG_CHUNK, SG_CHUNK), 0.5 * SG_CHUNK ** -0.5),
        "sg_b": gain(ks[6], (L, SG_GROUPS, SG_CHUNK)),
        "pool_w": nrm(ks[7], (L, POOL_GROUPS, POOL_GROUP_DIM, POOL_GROUP_DIM), POOL_GROUP_DIM ** -0.5),
        "pool_scale": gain(ks[8], (L, POOL_WIDTH)),
        "w_branch": nrm(ks[9], (L, N_BRANCHES, BRANCH_WIDTH, D_MODEL), BRANCH_WIDTH ** -0.5),
        "w_out": nrm(ks[10], (L, D_MODEL, D_MODEL), D_MODEL ** -0.5),
        "norm_post_mix": gain(ks[11], (L, D_MODEL)),
        "norm_pre_ffn": gain(ks[12], (L, D_MODEL)),
        "w_up": nrm(ks[13], (L, D_MODEL, 2 * D_FF), D_MODEL ** -0.5),
        "conv_w": nrm(ks[14], (L, CONV_WIDTH, 2 * D_FF), CONV_WIDTH ** -0.5),
        "conv_b": nrm(ks[15], (L, 2 * D_FF), 0.02),
        "w_down": nrm(ks[16], (L, D_FF, D_MODEL), D_FF ** -0.5),
        "norm_post_ffn": gain(ks[17], (L, D_MODEL)),
    }


def reference(x, norm_pre_mix, w_in, sg_ln_g, sg_ln_b, sg_w, sg_b, pool_w, pool_scale,
              w_branch, w_out, norm_post_mix, norm_pre_ffn, w_up, conv_w, conv_b, w_down,
              norm_post_ffn):
    B, S, D = x.shape
    to_heads = lambda t: t.reshape(B, S, SB_HEADS, SB_HEAD_DIM).transpose(0, 2, 1, 3)
    for l in range(DEPTH):
        h = rmsnorm(x, norm_pre_mix[l])
        proj = h @ w_in[l]
        q, k, v, u_sg, v_sg, x_pool, gate_logits = jnp.split(proj, IN_SPLITS, axis=-1)
        a = stick_breaking_attention(to_heads(q), to_heads(k), to_heads(v))
        a = a.transpose(0, 2, 1, 3).reshape(B, S, SB_WIDTH)
        b = spatial_gating(u_sg, v_sg, sg_w[l], sg_b[l], sg_ln_g[l], sg_ln_b[l])
        c = multiscale_pool(x_pool, pool_w[l], pool_scale[l])
        gates = jax.nn.sigmoid(gate_logits.reshape(B, S, N_BRANCHES, D))
        merged = (gates[:, :, 0] * (a @ w_branch[l, 0])
                  + gates[:, :, 1] * (b @ w_branch[l, 1])
                  + gates[:, :, 2] * (c @ w_branch[l, 2]))
        x = x + rmsnorm(merged @ w_out[l], norm_post_mix[l])
        h = rmsnorm(x, norm_pre_ffn[l])
        up = causal_depthwise_conv(h @ w_up[l], conv_w[l], conv_b[l])
        gt, val = jnp.split(up, 2, axis=-1)
        f = (jax.nn.gelu(gt, approximate=True) * val) @ w_down[l]
        x = x + rmsnorm(f, norm_post_ffn[l])
    return x
```

```python
import functools
import math

import jax
import jax.numpy as jnp
from jax import lax
from jax.experimental import pallas as pl
from jax.experimental.pallas import tpu as pltpu

D_MODEL = 1024
SB_HEADS = 8
SB_HEAD_DIM = 64
SB_WIDTH = SB_HEADS * SB_HEAD_DIM
SG_GROUPS = 4
SG_GROUP_DIM = 128
SG_WIDTH = SG_GROUPS * SG_GROUP_DIM
SG_CHUNK = 128
POOL_WINDOWS = (2, 4, 8, 16)
POOL_GROUP_DIM = 128
POOL_WIDTH = len(POOL_WINDOWS) * POOL_GROUP_DIM
POOL_TAIL = 16
D_FF = 2816
CONV_WIDTH = 3
CONV_TAIL = 8
EPS = 1e-6
QKV_WIDTH = 3 * SB_WIDTH
REST_WIDTH = 2 * SG_WIDTH + POOL_WIDTH + 3 * D_MODEL

LANES = 128
HEADS_PER_STEP = LANES // SB_HEAD_DIM
VMEM_LIMIT_BYTES = 56 * 1024 * 1024

F32 = jnp.float32
BF16 = jnp.bfloat16


def _dot(a, b):
    return jnp.dot(a, b, preferred_element_type=F32)


def _rmsnorm(x, g):
    return x * lax.rsqrt(jnp.mean(x * x, axis=-1, keepdims=True) + EPS) * g


def _gelu_tanh(x):
    c = math.sqrt(2.0 / math.pi)
    return x * (0.5 * (1.0 + jnp.tanh(c * (x + 0.044715 * (x * x * x)))))


def _sigmoid(x):
    return 1.0 / (1.0 + jnp.exp(-x))


def _const_spec(shape):
    return pl.BlockSpec(shape, lambda *_: (0,) * len(shape), pipeline_mode=pl.Buffered(1))


def _qkv_kernel(x_ref, g_ref, w_ref, o_ref):
    h = _rmsnorm(x_ref[...], g_ref[...]).astype(BF16)
    o_ref[...] = _dot(h, w_ref[...]).astype(BF16)


def _qkv_call(x2, g, w, tm):
    n = x2.shape[0]
    return pl.pallas_call(
        _qkv_kernel,
        grid=(n // tm,),
        in_specs=[pl.BlockSpec((tm, D_MODEL), lambda i: (i, 0)),
                  _const_spec((1, D_MODEL)),
                  _const_spec((D_MODEL, QKV_WIDTH))],
        out_specs=pl.BlockSpec((tm, QKV_WIDTH), lambda i: (i, 0)),
        out_shape=jax.ShapeDtypeStruct((n, QKV_WIDTH), BF16),
        compiler_params=pltpu.CompilerParams(
            dimension_semantics=("arbitrary",), vmem_limit_bytes=VMEM_LIMIT_BYTES),
        name="qkv_proj",
    )(x2, g, w)


def _attn_kernel(q_ref, k_ref, v_ref, o_ref, *, tq):
    i = pl.program_id(2)
    lane = lax.broadcasted_iota(jnp.int32, (1, LANES), 1)
    row = lax.broadcasted_iota(jnp.int32, (tq, tq), 0)
    col = lax.broadcasted_iota(jnp.int32, (tq, tq), 1)
    strict = col < row
    neg_suffix = jnp.where(row > col, -1.0, 0.0).astype(BF16)
    q = q_ref[...]

    def tile(j, head_mask, qh, diagonal):
        start = pl.multiple_of(j * tq, tq)
        kb = k_ref[pl.ds(start, tq), :]
        vb = jnp.where(head_mask, v_ref[pl.ds(start, tq), :], jnp.zeros((), BF16))
        z = lax.dot_general(qh, kb, (((1,), (1,)), ((), ())), preferred_element_type=F32)
        sp = jnp.maximum(z, 0.0) + jnp.log(1.0 + jnp.exp(-jnp.abs(z)))
        if diagonal:
            sp = jnp.where(strict, sp, 0.0)
        hi = sp.astype(BF16)
        lo = (sp - hi.astype(F32)).astype(BF16)
        r = _dot(jnp.concatenate([hi, lo], axis=0), neg_suffix)
        after = r[:tq] + r[tq:]
        w = jnp.exp(z - sp + after)
        if diagonal:
            w = jnp.where(strict, w, 0.0)
        pv = _dot(w.astype(BF16), vb)
        return pv, jnp.sum(sp, axis=-1, keepdims=True)

    out = None
    for hh in range(HEADS_PER_STEP):
        head_mask = (lane >= hh * SB_HEAD_DIM) & (lane < (hh + 1) * SB_HEAD_DIM)
        qh = jnp.where(head_mask, q, jnp.zeros((), BF16))
        pv0, c0 = tile(i, head_mask, qh, True)

        def body(jj, carry, head_mask=head_mask, qh=qh):
            c, acc = carry
            pv, rs = tile(i - 1 - jj, head_mask, qh, False)
            return c + rs, acc + jnp.exp(-c) * pv

        _, acc = lax.fori_loop(0, i, body, (c0, pv0))
        out = acc if out is None else out + acc
    o_ref[...] = out.astype(BF16)


def _attn_call(qkv3, tq):
    b, s, _ = qkv3.shape
    n_pairs = SB_HEADS // HEADS_PER_STEP
    return pl.pallas_call(
        functools.partial(_attn_kernel, tq=tq),
        grid=(b, n_pairs, s // tq),
        in_specs=[pl.BlockSpec((None, tq, LANES), lambda bi, hp, i: (bi, i, hp)),
                  pl.BlockSpec((None, s, LANES), lambda bi, hp, i: (bi, 0, n_pairs + hp)),
                  pl.BlockSpec((None, s, LANES), lambda bi, hp, i: (bi, 0, 2 * n_pairs + hp))],
        out_specs=pl.BlockSpec((None, tq, LANES), lambda bi, hp, i: (bi, i, hp)),
        out_shape=jax.ShapeDtypeStruct((b, s, SB_WIDTH), BF16),
        compiler_params=pltpu.CompilerParams(
            dimension_semantics=("arbitrary", "arbitrary", "arbitrary"),
            vmem_limit_bytes=VMEM_LIMIT_BYTES),
        name="stickbreak_attn",
    )(qkv3, qkv3, qkv3)


def _mixer_kernel(x_ref, a_ref, gpre_ref, wr_ref, lng_ref, lnb_ref, sgw_ref, sgb_ref, pw_ref, ps_ref,
                  wb_ref, wo_ref, gpost_ref, o_ref, tail_ref, bsg_ref, *, tm, tiles_per_seq):
    t = pl.program_id(0)
    seq_tile = t % tiles_per_seq
    x = x_ref[...]
    h = _rmsnorm(x, gpre_ref[...]).astype(BF16)

    o_u, o_v, o_p, o_g = 0, SG_WIDTH, 2 * SG_WIDTH, 2 * SG_WIDTH + POOL_WIDTH
    u = _gelu_tanh(_dot(h, wr_ref[:, o_u:o_u + SG_WIDTH]))
    v = _gelu_tanh(_dot(h, wr_ref[:, o_v:o_v + SG_WIDTH]))
    mu = jnp.mean(v, axis=-1, keepdims=True)
    vc = v - mu
    var = jnp.mean(vc * vc, axis=-1, keepdims=True)
    vn = (vc * lax.rsqrt(var + EPS) * lng_ref[...] + lnb_ref[...]).astype(BF16)
    r_i = lax.broadcasted_iota(jnp.int32, (SG_CHUNK, SG_CHUNK), 0)
    c_i = lax.broadcasted_iota(jnp.int32, (SG_CHUNK, SG_CHUNK), 1)
    for g in range(SG_GROUPS):
        wg = jnp.where(c_i <= r_i, sgw_ref[g], 0.0).astype(BF16)
        lo = g * SG_GROUP_DIM
        for n in range(tm // SG_CHUNK):
            rows = slice(n * SG_CHUNK, (n + 1) * SG_CHUNK)
            mixed = _dot(wg, vn[rows, lo:lo + SG_GROUP_DIM]) + sgb_ref[g]
            bsg_ref[rows, lo:lo + SG_GROUP_DIM] = (u[rows, lo:lo + SG_GROUP_DIM] * mixed).astype(BF16)

    xp = _dot(h, wr_ref[:, o_p:o_p + POOL_WIDTH])

    @pl.when(seq_tile == 0)
    def _():
        tail_ref[...] = jnp.zeros_like(tail_ref)

    ext = jnp.concatenate([tail_ref[...], xp], axis=0)
    tail_ref[...] = xp[tm - POOL_TAIL:, :]
    pos = seq_tile * tm + lax.broadcasted_iota(jnp.int32, (tm, POOL_GROUP_DIM), 0)
    c_parts = []
    for g, win in enumerate(POOL_WINDOWS):
        lo = g * POOL_GROUP_DIM
        s = ext[:, lo:lo + POOL_GROUP_DIM]
        span = 1
        while span < win:
            s = s + pltpu.roll(s, span, 0)
            span *= 2
        count = jnp.minimum(pos + 1, win).astype(F32)
        pooled = s[POOL_TAIL:, :] / count - xp[:, lo:lo + POOL_GROUP_DIM]
        c_parts.append(_dot(pooled.astype(BF16), pw_ref[g]))
    cpool = (jnp.concatenate(c_parts, axis=1) * ps_ref[...]).astype(BF16)

    merged = _sigmoid(_dot(h, wr_ref[:, o_g:o_g + D_MODEL])) * _dot(a_ref[...], wb_ref[0])
    merged += _sigmoid(_dot(h, wr_ref[:, o_g + D_MODEL:o_g + 2 * D_MODEL])) * _dot(bsg_ref[...], wb_ref[1])
    merged += _sigmoid(_dot(h, wr_ref[:, o_g + 2 * D_MODEL:o_g + 3 * D_MODEL])) * _dot(cpool, wb_ref[2])
    y = _dot(merged.astype(BF16), wo_ref[...])
    o_ref[...] = x + _rmsnorm(y, gpost_ref[...])


def _mixer_call(x2, a2, gpre, wr, lng, lnb, sgw, sgb, pw, ps, wb, wo, gpost, tm, seq):
    n = x2.shape[0]
    return pl.pallas_call(
        functools.partial(_mixer_kernel, tm=tm, tiles_per_seq=seq // tm),
        grid=(n // tm,),
        in_specs=[pl.BlockSpec((tm, D_MODEL), lambda i: (i, 0)),
                  pl.BlockSpec((tm, SB_WIDTH), lambda i: (i, 0)),
                  _const_spec((1, D_MODEL)),
                  _const_spec((D_MODEL, REST_WIDTH)),
                  _const_spec((1, SG_WIDTH)),
                  _const_spec((1, SG_WIDTH)),
                  _const_spec((SG_GROUPS, SG_CHUNK, SG_CHUNK)),
                  _const_spec((SG_GROUPS, SG_CHUNK, 1)),
                  _const_spec((len(POOL_WINDOWS), POOL_GROUP_DIM, POOL_GROUP_DIM)),
                  _const_spec((1, POOL_WIDTH)),
                  _const_spec((3, SB_WIDTH, D_MODEL)),
                  _const_spec((D_MODEL, D_MODEL)),
                  _const_spec((1, D_MODEL))],
        out_specs=pl.BlockSpec((tm, D_MODEL), lambda i: (i, 0)),
        out_shape=jax.ShapeDtypeStruct((n, D_MODEL), F32),
        scratch_shapes=[pltpu.VMEM((POOL_TAIL, POOL_WIDTH), F32),
                        pltpu.VMEM((tm, SG_WIDTH), BF16)],
        compiler_params=pltpu.CompilerParams(
            dimension_semantics=("arbitrary",), vmem_limit_bytes=VMEM_LIMIT_BYTES),
        name="mixer",
    )(x2, a2, gpre, wr, lng, lnb, sgw, sgb, pw, ps, wb, wo, gpost)


def _ffn_chunks():
    sizes, left = [], D_FF
    while left:
        sizes.append(min(512, left))
        left -= sizes[-1]
    return sizes


def _ffn_kernel(x_ref, gpre_ref, wup_ref, cw_ref, cb_ref, wdn_ref, gpost_ref, o_ref, tail_ref,
                *, tm, tiles_per_seq):
    t = pl.program_id(0)
    x = x_ref[...]
    h = _rmsnorm(x, gpre_ref[...]).astype(BF16)

    @pl.when(t % tiles_per_seq == 0)
    def _():
        tail_ref[...] = jnp.zeros_like(tail_ref)

    def conv_up(lo, width):
        cols = slice(lo, lo + width)
        up = _dot(h, wup_ref[:, cols])
        ext = jnp.concatenate([tail_ref[:, cols], up], axis=0)
        tail_ref[:, cols] = up[tm - CONV_TAIL:, :]
        y = cb_ref[:, cols] + cw_ref[2:3, cols] * ext
        y = y + cw_ref[1:2, cols] * pltpu.roll(ext, 1, 0)
        y = y + cw_ref[0:1, cols] * pltpu.roll(ext, 2, 0)
        return y[CONV_TAIL:, :]

    acc = jnp.zeros((tm, D_MODEL), F32)
    lo = 0
    for width in _ffn_chunks():
        f = _gelu_tanh(conv_up(lo, width)) * conv_up(D_FF + lo, width)
        acc = acc + _dot(f.astype(BF16), wdn_ref[lo:lo + width, :])
        lo += width
    o_ref[...] = x + _rmsnorm(acc, gpost_ref[...])


def _ffn_call(x2, gpre, wup, cw, cb, wdn, gpost, tm, seq):
    n = x2.shape[0]
    return pl.pallas_call(
        functools.partial(_ffn_kernel, tm=tm, tiles_per_seq=seq // tm),
        grid=(n // tm,),
        in_specs=[pl.BlockSpec((tm, D_MODEL), lambda i: (i, 0)),
                  _const_spec((1, D_MODEL)),
                  _const_spec((D_MODEL, 2 * D_FF)),
                  _const_spec((CONV_WIDTH, 2 * D_FF)),
                  _const_spec((1, 2 * D_FF)),
                  _const_spec((D_FF, D_MODEL)),
                  _const_spec((1, D_MODEL))],
        out_specs=pl.BlockSpec((tm, D_MODEL), lambda i: (i, 0)),
        out_shape=jax.ShapeDtypeStruct((n, D_MODEL), F32),
        scratch_shapes=[pltpu.VMEM((CONV_TAIL, 2 * D_FF), F32)],
        compiler_params=pltpu.CompilerParams(
            dimension_semantics=("arbitrary",), vmem_limit_bytes=VMEM_LIMIT_BYTES),
        name="geglu_ffn",
    )(x2, gpre, wup, cw, cb, wdn, gpost)


def _tile(seq, want):
    tile = min(want, seq)
    assert seq % tile == 0 and tile % SG_CHUNK == 0, (seq, tile)
    return tile


def kernel(x, norm_pre_mix, w_in, sg_ln_g, sg_ln_b, sg_w, sg_b, pool_w, pool_scale, w_branch, w_out,
           norm_post_mix, norm_pre_ffn, w_up, conv_w, conv_b, w_down, norm_post_ffn):
    b, s, d = x.shape
    assert d == D_MODEL
    depth = w_in.shape[0]
    tm = _tile(s, 512)
    tq = _tile(s, 256)
    row = lambda p: p.reshape(1, -1)
    q_scale = jnp.concatenate([jnp.full((SB_WIDTH,), SB_HEAD_DIM ** -0.5, F32),
                               jnp.ones((QKV_WIDTH - SB_WIDTH,), F32)])
    x2 = x.reshape(b * s, d)
    for l in range(depth):
        w_qkv = (w_in[l, :, :QKV_WIDTH] * q_scale).astype(BF16)
        w_rest = w_in[l, :, QKV_WIDTH:].astype(BF16)
        qkv = _qkv_call(x2, row(norm_pre_mix[l]), w_qkv, tm)
        a = _attn_call(qkv.reshape(b, s, QKV_WIDTH), tq)
        x2 = _mixer_call(x2, a.reshape(b * s, SB_WIDTH), row(norm_pre_mix[l]), w_rest,
                         row(sg_ln_g[l]), row(sg_ln_b[l]), sg_w[l], sg_b[l].reshape(SG_GROUPS, SG_CHUNK, 1),
                         pool_w[l].astype(BF16), row(pool_scale[l]), w_branch[l].astype(BF16),
                         w_out[l].astype(BF16), row(norm_post_mix[l]), tm, s)
        x2 = _ffn_call(x2, row(norm_pre_ffn[l]), w_up[l].astype(BF16), conv_w[l], row(conv_b[l]),
                       w_down[l].astype(BF16), row(norm_post_ffn[l]), tm, s)
    return x2.reshape(b, s, d)
```

```python
import functools
import math

import jax
import jax.numpy as jnp
from jax import lax
from jax.experimental import pallas as pl
from jax.experimental.pallas import tpu as pltpu

D_MODEL = 1024
SB_HEADS = 8
SB_HEAD_DIM = 64
SB_WIDTH = SB_HEADS * SB_HEAD_DIM
SG_GROUPS = 4
SG_GROUP_DIM = 128
SG_WIDTH = SG_GROUPS * SG_GROUP_DIM
SG_CHUNK = 128
POOL_WINDOWS = (2, 4, 8, 16)
POOL_GROUP_DIM = 128
POOL_WIDTH = len(POOL_WINDOWS) * POOL_GROUP_DIM
POOL_TAIL = 16
D_FF = 2816
CONV_WIDTH = 3
CONV_TAIL = 8
EPS = 1e-6
QKV_WIDTH = 3 * SB_WIDTH
REST_WIDTH = 2 * SG_WIDTH + POOL_WIDTH + 3 * D_MODEL

LANES = 128
HEADS_PER_STEP = LANES // SB_HEAD_DIM
VMEM_LIMIT_BYTES = 56 * 1024 * 1024

F32 = jnp.float32
BF16 = jnp.bfloat16


def _dot(a, b):
    return jnp.dot(a, b, preferred_element_type=F32)


def _rmsnorm(x, g):
    return x * lax.rsqrt(jnp.mean(x * x, axis=-1, keepdims=True) + EPS) * g


def _gelu_tanh(x):
    c = math.sqrt(2.0 / math.pi)
    return x * (0.5 * (1.0 + jnp.tanh(c * (x + 0.044715 * (x * x * x)))))


def _sigmoid(x):
    return 1.0 / (1.0 + jnp.exp(-x))


def _const_spec(shape):
    return pl.BlockSpec(shape, lambda *_: (0,) * len(shape), pipeline_mode=pl.Buffered(1))


Q_SCALE = math.log2(math.e) * SB_HEAD_DIM ** -0.5
LN2_INV = math.log2(math.e)


def _qkv_kernel(x_ref, g_ref, w_ref, o_ref):
    h = _rmsnorm(x_ref[...], g_ref[...]).astype(BF16)
    o_ref[:, :SB_WIDTH] = (_dot(h, w_ref[:, :SB_WIDTH]) * Q_SCALE).astype(BF16)
    o_ref[:, SB_WIDTH:] = _dot(h, w_ref[:, SB_WIDTH:]).astype(BF16)


def _qkv_call(x2, g, w, tm):
    n = x2.shape[0]
    return pl.pallas_call(
        _qkv_kernel,
        grid=(n // tm,),
        in_specs=[pl.BlockSpec((tm, D_MODEL), lambda i: (i, 0)),
                  _const_spec((1, D_MODEL)),
                  _const_spec((D_MODEL, QKV_WIDTH))],
        out_specs=pl.BlockSpec((tm, QKV_WIDTH), lambda i: (i, 0)),
        out_shape=jax.ShapeDtypeStruct((n, QKV_WIDTH), BF16),
        compiler_params=pltpu.CompilerParams(
            dimension_semantics=("arbitrary",), vmem_limit_bytes=VMEM_LIMIT_BYTES),
        name="qkv_proj",
    )(x2, g, w)


MASKED_LOGIT = -1e30


def _attn_kernel(q_ref, k_ref, v_ref, o_ref, acc_ref, c_ref, scale_ref, hl_ref, d_ref, w_ref, rs_new_ref,
                 rs_old_ref, *, tq, kb):
    i = pl.program_id(2)
    n_full = 2 * i
    lane = lax.broadcasted_iota(jnp.int32, (1, LANES), 1)
    head_masks = [(lane >= hh * SB_HEAD_DIM) & (lane < (hh + 1) * SB_HEAD_DIM) for hh in range(HEADS_PER_STEP)]
    r2 = lax.broadcasted_iota(jnp.int32, (2 * kb, kb), 0)
    c2 = lax.broadcasted_iota(jnp.int32, (2 * kb, kb), 1)
    neg_suffix = jnp.where((r2 & (kb - 1)) > c2, -1.0, 0.0).astype(BF16)
    strict = (lax.broadcasted_iota(jnp.int32, (tq, kb), 1) < lax.broadcasted_iota(jnp.int32, (tq, kb), 0))
    q = q_ref[...]
    q_heads = [jnp.where(m, q, jnp.zeros((), BF16)) for m in head_masks]
    all_rows = slice(0, tq)
    late_rows = slice(kb, tq)

    def key_start(t):
        return pl.multiple_of((n_full + 1 - t) * kb, kb)

    def scores(t, slot, rows, mask):
        kblk = k_ref[pl.ds(key_start(t), kb), :]
        for hh in range(HEADS_PER_STEP):
            z = lax.dot_general(q_heads[hh][rows], kblk, (((1,), (1,)), ((), ())), preferred_element_type=F32)
            sp = jnp.maximum(z, 0.0) + jnp.log(1.0 + jnp.exp2(-jnp.abs(z))) * LN2_INV
            d = z - sp
            if mask is not None:
                sp = jnp.where(mask, sp, 0.0)
                d = jnp.where(mask, d, MASKED_LOGIT)
            hi = sp.astype(BF16)
            hl_ref[slot, hh, rows, :kb] = hi
            hl_ref[slot, hh, rows, kb:] = (sp - hi.astype(F32)).astype(BF16)
            d_ref[slot, hh, rows] = d
            rs_new_ref[hh, rows] = jnp.sum(sp, axis=-1, keepdims=True)

    def weights(slot, rows):
        for hh in range(HEADS_PER_STEP):
            suffix = _dot(hl_ref[slot, hh, rows], neg_suffix)
            w_ref[slot, hh, rows] = jnp.exp2(d_ref[slot, hh, rows] + suffix).astype(BF16)

    def output(t, slot):
        vblk = v_ref[pl.ds(key_start(t), kb), :]
        for hh in range(HEADS_PER_STEP):
            pv = _dot(w_ref[slot, hh], jnp.where(head_masks[hh], vblk, jnp.zeros((), BF16)))
            acc_ref[hh] += scale_ref[hh] * pv

    def advance_carry():
        for hh in range(HEADS_PER_STEP):
            c = c_ref[hh]
            scale_ref[hh] = jnp.exp2(-c)
            c_ref[hh] = c + rs_old_ref[hh]
            rs_old_ref[hh] = rs_new_ref[hh]

    acc_ref[...] = jnp.zeros_like(acc_ref)
    c_ref[...] = jnp.zeros_like(c_ref)
    rs_old_ref[...] = jnp.zeros_like(rs_old_ref)
    rs_new_ref[:, :kb] = jnp.zeros((HEADS_PER_STEP, kb, 1), F32)
    w_ref[0, :, :kb] = jnp.zeros((HEADS_PER_STEP, kb, kb), BF16)
    scores(0, 0, late_rows, strict[:kb])
    advance_carry()
    scores(1, 1, all_rows, strict)
    weights(0, late_rows)

    def body(jj, carry):
        t = 2 + 2 * jj
        for slot in range(2):
            advance_carry()
            scores(t + slot, slot, all_rows, None)
            weights(1 - slot, all_rows)
            output(t + slot - 2, slot)
        return carry

    lax.fori_loop(0, i, body, 0)
    advance_carry()
    weights(1, all_rows)
    output(n_full, 0)
    advance_carry()
    output(n_full + 1, 1)
    o_ref[...] = (acc_ref[0] + acc_ref[1]).astype(BF16)


def _attn_call(qkv3, tq):
    b, s, _ = qkv3.shape
    n_pairs = SB_HEADS // HEADS_PER_STEP
    return pl.pallas_call(
        functools.partial(_attn_kernel, tq=tq, kb=tq // 2),
        grid=(b, n_pairs, s // tq),
        in_specs=[pl.BlockSpec((None, tq, LANES), lambda bi, hp, i: (bi, i, hp)),
                  pl.BlockSpec((None, s, LANES), lambda bi, hp, i: (bi, 0, n_pairs + hp)),
                  pl.BlockSpec((None, s, LANES), lambda bi, hp, i: (bi, 0, 2 * n_pairs + hp))],
        out_specs=pl.BlockSpec((None, tq, LANES), lambda bi, hp, i: (bi, i, hp)),
        out_shape=jax.ShapeDtypeStruct((b, s, SB_WIDTH), BF16),
        scratch_shapes=[pltpu.VMEM((HEADS_PER_STEP, tq, LANES), F32),
                        pltpu.VMEM((HEADS_PER_STEP, tq, 1), F32),
                        pltpu.VMEM((HEADS_PER_STEP, tq, 1), F32),
                        pltpu.VMEM((2, HEADS_PER_STEP, tq, tq), BF16),
                        pltpu.VMEM((2, HEADS_PER_STEP, tq, tq // 2), F32),
                        pltpu.VMEM((2, HEADS_PER_STEP, tq, tq // 2), BF16),
                        pltpu.VMEM((HEADS_PER_STEP, tq, 1), F32),
                        pltpu.VMEM((HEADS_PER_STEP, tq, 1), F32)],
        compiler_params=pltpu.CompilerParams(
            dimension_semantics=("arbitrary", "arbitrary", "arbitrary"),
            vmem_limit_bytes=VMEM_LIMIT_BYTES),
        name="stickbreak_attn",
    )(qkv3, qkv3, qkv3)


def _mixer_kernel(x_ref, a_ref, gpre_ref, wr_ref, lng_ref, lnb_ref, sgw_ref, sgb_ref, pw_ref, ps_ref,
                  wb_ref, wo_ref, gpost_ref, o_ref, tail_ref, bsg_ref, *, tm, tiles_per_seq):
    t = pl.program_id(0)
    seq_tile = t % tiles_per_seq
    x = x_ref[...]
    h = _rmsnorm(x, gpre_ref[...]).astype(BF16)

    o_u, o_v, o_p, o_g = 0, SG_WIDTH, 2 * SG_WIDTH, 2 * SG_WIDTH + POOL_WIDTH
    u = _gelu_tanh(_dot(h, wr_ref[:, o_u:o_u + SG_WIDTH]))
    v = _gelu_tanh(_dot(h, wr_ref[:, o_v:o_v + SG_WIDTH]))
    mu = jnp.mean(v, axis=-1, keepdims=True)
    vc = v - mu
    var = jnp.mean(vc * vc, axis=-1, keepdims=True)
    vn = (vc * lax.rsqrt(var + EPS) * lng_ref[...] + lnb_ref[...]).astype(BF16)
    r_i = lax.broadcasted_iota(jnp.int32, (SG_CHUNK, SG_CHUNK), 0)
    c_i = lax.broadcasted_iota(jnp.int32, (SG_CHUNK, SG_CHUNK), 1)
    for g in range(SG_GROUPS):
        wg = jnp.where(c_i <= r_i, sgw_ref[g], 0.0).astype(BF16)
        lo = g * SG_GROUP_DIM
        for n in range(tm // SG_CHUNK):
            rows = slice(n * SG_CHUNK, (n + 1) * SG_CHUNK)
            mixed = _dot(wg, vn[rows, lo:lo + SG_GROUP_DIM]) + sgb_ref[g]
            bsg_ref[rows, lo:lo + SG_GROUP_DIM] = (u[rows, lo:lo + SG_GROUP_DIM] * mixed).astype(BF16)

    xp = _dot(h, wr_ref[:, o_p:o_p + POOL_WIDTH])

    @pl.when(seq_tile == 0)
    def _():
        tail_ref[...] = jnp.zeros_like(tail_ref)

    ext = jnp.concatenate([tail_ref[...], xp], axis=0)
    tail_ref[...] = xp[tm - POOL_TAIL:, :]
    pos = seq_tile * tm + lax.broadcasted_iota(jnp.int32, (tm, POOL_GROUP_DIM), 0)
    c_parts = []
    for g, win in enumerate(POOL_WINDOWS):
        lo = g * POOL_GROUP_DIM
        s = ext[:, lo:lo + POOL_GROUP_DIM]
        span = 1
        while span < win:
            s = s + pltpu.roll(s, span, 0)
            span *= 2
        count = jnp.minimum(pos + 1, win).astype(F32)
        pooled = s[POOL_TAIL:, :] / count - xp[:, lo:lo + POOL_GROUP_DIM]
        c_parts.append(_dot(pooled.astype(BF16), pw_ref[g]))
    cpool = (jnp.concatenate(c_parts, axis=1) * ps_ref[...]).astype(BF16)

    merged = _sigmoid(_dot(h, wr_ref[:, o_g:o_g + D_MODEL])) * _dot(a_ref[...], wb_ref[0])
    merged += _sigmoid(_dot(h, wr_ref[:, o_g + D_MODEL:o_g + 2 * D_MODEL])) * _dot(bsg_ref[...], wb_ref[1])
    merged += _sigmoid(_dot(h, wr_ref[:, o_g + 2 * D_MODEL:o_g + 3 * D_MODEL])) * _dot(cpool, wb_ref[2])
    y = _dot(merged.astype(BF16), wo_ref[...])
    o_ref[...] = x + _rmsnorm(y, gpost_ref[...])


def _mixer_call(x2, a2, gpre, wr, lng, lnb, sgw, sgb, pw, ps, wb, wo, gpost, tm, seq):
    n = x2.shape[0]
    return pl.pallas_call(
        functools.partial(_mixer_kernel, tm=tm, tiles_per_seq=seq // tm),
        grid=(n // tm,),
        in_specs=[pl.BlockSpec((tm, D_MODEL), lambda i: (i, 0)),
                  pl.BlockSpec((tm, SB_WIDTH), lambda i: (i, 0)),
                  _const_spec((1, D_MODEL)),
                  _const_spec((D_MODEL, REST_WIDTH)),
                  _const_spec((1, SG_WIDTH)),
                  _const_spec((1, SG_WIDTH)),
                  _const_spec((SG_GROUPS, SG_CHUNK, SG_CHUNK)),
                  _const_spec((SG_GROUPS, SG_CHUNK, 1)),
                  _const_spec((len(POOL_WINDOWS), POOL_GROUP_DIM, POOL_GROUP_DIM)),
                  _const_spec((1, POOL_WIDTH)),
                  _const_spec((3, SB_WIDTH, D_MODEL)),
                  _const_spec((D_MODEL, D_MODEL)),
                  _const_spec((1, D_MODEL))],
        out_specs=pl.BlockSpec((tm, D_MODEL), lambda i: (i, 0)),
        out_shape=jax.ShapeDtypeStruct((n, D_MODEL), F32),
        scratch_shapes=[pltpu.VMEM((POOL_TAIL, POOL_WIDTH), F32),
                        pltpu.VMEM((tm, SG_WIDTH), BF16)],
        compiler_params=pltpu.CompilerParams(
            dimension_semantics=("arbitrary",), vmem_limit_bytes=VMEM_LIMIT_BYTES),
        name="mixer",
    )(x2, a2, gpre, wr, lng, lnb, sgw, sgb, pw, ps, wb, wo, gpost)


def _ffn_chunks():
    sizes, left = [], D_FF
    while left:
        sizes.append(min(512, left))
        left -= sizes[-1]
    return sizes


def _ffn_kernel(x_ref, gpre_ref, wup_ref, cw_ref, cb_ref, wdn_ref, gpost_ref, o_ref, tail_ref,
                *, tm, tiles_per_seq):
    t = pl.program_id(0)
    x = x_ref[...]
    h = _rmsnorm(x, gpre_ref[...]).astype(BF16)

    @pl.when(t % tiles_per_seq == 0)
    def _():
        tail_ref[...] = jnp.zeros_like(tail_ref)

    def conv_up(lo, width):
        cols = slice(lo, lo + width)
        up = _dot(h, wup_ref[:, cols])
        ext = jnp.concatenate([tail_ref[:, cols], up], axis=0)
        tail_ref[:, cols] = up[tm - CONV_TAIL:, :]
        y = cb_ref[:, cols] + cw_ref[2:3, cols] * ext
        y = y + cw_ref[1:2, cols] * pltpu.roll(ext, 1, 0)
        y = y + cw_ref[0:1, cols] * pltpu.roll(ext, 2, 0)
        return y[CONV_TAIL:, :]

    acc = jnp.zeros((tm, D_MODEL), F32)
    lo = 0
    for width in _ffn_chunks():
        f = _gelu_tanh(conv_up(lo, width)) * conv_up(D_FF + lo, width)
        acc = acc + _dot(f.astype(BF16), wdn_ref[lo:lo + width, :])
        lo += width
    o_ref[...] = x + _rmsnorm(acc, gpost_ref[...])


def _ffn_call(x2, gpre, wup, cw, cb, wdn, gpost, tm, seq):
    n = x2.shape[0]
    return pl.pallas_call(
        functools.partial(_ffn_kernel, tm=tm, tiles_per_seq=seq // tm),
        grid=(n // tm,),
        in_specs=[pl.BlockSpec((tm, D_MODEL), lambda i: (i, 0)),
                  _const_spec((1, D_MODEL)),
                  _const_spec((D_MODEL, 2 * D_FF)),
                  _const_spec((CONV_WIDTH, 2 * D_FF)),
                  _const_spec((1, 2 * D_FF)),
                  _const_spec((D_FF, D_MODEL)),
                  _const_spec((1, D_MODEL))],
        out_specs=pl.BlockSpec((tm, D_MODEL), lambda i: (i, 0)),
        out_shape=jax.ShapeDtypeStruct((n, D_MODEL), F32),
        scratch_shapes=[pltpu.VMEM((CONV_TAIL, 2 * D_FF), F32)],
        compiler_params=pltpu.CompilerParams(
            dimension_semantics=("arbitrary",), vmem_limit_bytes=VMEM_LIMIT_BYTES),
        name="geglu_ffn",
    )(x2, gpre, wup, cw, cb, wdn, gpost)


def _tile(seq, want):
    tile = min(want, seq)
    assert seq % tile == 0 and tile % SG_CHUNK == 0, (seq, tile)
    return tile


def kernel(x, norm_pre_mix, w_in, sg_ln_g, sg_ln_b, sg_w, sg_b, pool_w, pool_scale, w_branch, w_out,
           norm_post_mix, norm_pre_ffn, w_up, conv_w, conv_b, w_down, norm_post_ffn):
    b, s, d = x.shape
    assert d == D_MODEL
    depth = w_in.shape[0]
    tm = _tile(s, 512)
    tq = _tile(s, 512)
    row = lambda p: p.reshape(1, -1)
    x2 = x.reshape(b * s, d)
    for l in range(depth):
        w_qkv = w_in[l, :, :QKV_WIDTH].astype(BF16)
        w_rest = w_in[l, :, QKV_WIDTH:].astype(BF16)
        qkv = _qkv_call(x2, row(norm_pre_mix[l]), w_qkv, tm)
        a = _attn_call(qkv.reshape(b, s, QKV_WIDTH), tq)
        x2 = _mixer_call(x2, a.reshape(b * s, SB_WIDTH), row(norm_pre_mix[l]), w_rest,
                         row(sg_ln_g[l]), row(sg_ln_b[l]), sg_w[l], sg_b[l].reshape(SG_GROUPS, SG_CHUNK, 1),
                         pool_w[l].astype(BF16), row(pool_scale[l]), w_branch[l].astype(BF16),
                         w_out[l].astype(BF16), row(norm_post_mix[l]), tm, s)
        x2 = _ffn_call(x2, row(norm_pre_ffn[l]), w_up[l].astype(BF16), conv_w[l], row(conv_b[l]),
                       w_down[l].astype(BF16), row(norm_post_ffn[l]), tm, s)
    return x2.reshape(b, s, d)
```

```python
import functools
import math

import jax
import jax.numpy as jnp
from jax import lax
from jax.experimental import pallas as pl
from jax.experimental.pallas import tpu as pltpu

D_MODEL = 1024
SB_HEADS = 8
SB_HEAD_DIM = 64
SB_WIDTH = SB_HEADS * SB_HEAD_DIM
SG_GROUPS = 4
SG_GROUP_DIM = 128
SG_WIDTH = SG_GROUPS * SG_GROUP_DIM
SG_CHUNK = 128
POOL_WINDOWS = (2, 4, 8, 16)
POOL_GROUP_DIM = 128
POOL_WIDTH = len(POOL_WINDOWS) * POOL_GROUP_DIM
POOL_TAIL = 16
D_FF = 2816
CONV_WIDTH = 3
CONV_TAIL = 8
EPS = 1e-6
QKV_WIDTH = 3 * SB_WIDTH
REST_WIDTH = 2 * SG_WIDTH + POOL_WIDTH + 3 * D_MODEL

LANES = 128
HEADS_PER_STEP = LANES // SB_HEAD_DIM
VMEM_LIMIT_BYTES = 56 * 1024 * 1024

F32 = jnp.float32
BF16 = jnp.bfloat16


def _dot(a, b):
    return jnp.dot(a, b, preferred_element_type=F32)


def _rmsnorm(x, g):
    return x * lax.rsqrt(jnp.mean(x * x, axis=-1, keepdims=True) + EPS) * g


def _gelu_tanh(x):
    c = math.sqrt(2.0 / math.pi)
    return x * (0.5 * (1.0 + jnp.tanh(c * (x + 0.044715 * (x * x * x)))))


def _sigmoid(x):
    return 1.0 / (1.0 + jnp.exp(-x))


def _const_spec(shape):
    return pl.BlockSpec(shape, lambda *_: (0,) * len(shape), pipeline_mode=pl.Buffered(1))


Q_SCALE = math.log2(math.e) * SB_HEAD_DIM ** -0.5
LN2_INV = math.log2(math.e)


def _qkv_kernel(x_ref, g_ref, w_ref, o_ref):
    h = _rmsnorm(x_ref[...], g_ref[...]).astype(BF16)
    o_ref[:, :SB_WIDTH] = (_dot(h, w_ref[:, :SB_WIDTH]) * Q_SCALE).astype(BF16)
    o_ref[:, SB_WIDTH:] = _dot(h, w_ref[:, SB_WIDTH:]).astype(BF16)


def _qkv_call(x2, g, w, tm):
    n = x2.shape[0]
    return pl.pallas_call(
        _qkv_kernel,
        grid=(n // tm,),
        in_specs=[pl.BlockSpec((tm, D_MODEL), lambda i: (i, 0)),
                  _const_spec((1, D_MODEL)),
                  _const_spec((D_MODEL, QKV_WIDTH))],
        out_specs=pl.BlockSpec((tm, QKV_WIDTH), lambda i: (i, 0)),
        out_shape=jax.ShapeDtypeStruct((n, QKV_WIDTH), BF16),
        compiler_params=pltpu.CompilerParams(
            dimension_semantics=("arbitrary",), vmem_limit_bytes=VMEM_LIMIT_BYTES),
        name="qkv_proj",
    )(x2, g, w)


MASKED_LOGIT = -1e30


def _attn_kernel(q_ref, k_ref, v_ref, o_ref, acc_ref, c_ref, scale_ref, sp_ref, z_ref, w_ref, rs_new_ref,
                 rs_old_ref, vh_ref, *, tq, kb):
    i = pl.program_id(2)
    n_full = 2 * i
    lane = lax.broadcasted_iota(jnp.int32, (1, LANES), 1)
    head_masks = [(lane >= hh * SB_HEAD_DIM) & (lane < (hh + 1) * SB_HEAD_DIM) for hh in range(HEADS_PER_STEP)]
    r2 = lax.broadcasted_iota(jnp.int32, (kb, kb), 0)
    c2 = lax.broadcasted_iota(jnp.int32, (kb, kb), 1)
    neg_suffix = jnp.where(r2 >= c2, -1.0, 0.0).astype(BF16)
    strict = (lax.broadcasted_iota(jnp.int32, (tq, kb), 1) < lax.broadcasted_iota(jnp.int32, (tq, kb), 0))
    q = q_ref[...]
    q_heads = [jnp.where(m, q, jnp.zeros((), BF16)) for m in head_masks]
    all_rows = slice(0, tq)
    late_rows = slice(kb, tq)

    def key_start(t):
        return pl.multiple_of((n_full + 1 - t) * kb, kb)

    def scores(t, slot, rows, mask):
        kblk = k_ref[pl.ds(key_start(t), kb), :]
        for hh in range(HEADS_PER_STEP):
            z = lax.dot_general(q_heads[hh][rows], kblk, (((1,), (1,)), ((), ())), preferred_element_type=F32)
            sp = jnp.maximum(z, 0.0) + jnp.log(1.0 + jnp.exp2(-jnp.abs(z))) * LN2_INV
            if mask is not None:
                sp = jnp.where(mask, sp, 0.0)
                z = jnp.where(mask, z, MASKED_LOGIT)
            sp_ref[slot, hh, rows] = sp.astype(BF16)
            z_ref[slot, hh, rows] = z
            rs_new_ref[hh, rows] = jnp.broadcast_to(jnp.sum(sp, axis=-1, keepdims=True), (sp.shape[0], LANES))

    def weights(slot, rows):
        for hh in range(HEADS_PER_STEP):
            suffix = _dot(sp_ref[slot, hh, rows], neg_suffix)
            w_ref[slot, hh, rows] = jnp.exp2(z_ref[slot, hh, rows] + suffix).astype(BF16)

    def output(t, slot):
        for hh in range(HEADS_PER_STEP):
            pv = _dot(w_ref[slot, hh], vh_ref[hh, pl.ds(key_start(t), kb), :])
            acc_ref[hh] += scale_ref[slot, hh] * pv

    def advance_carry(slot):
        for hh in range(HEADS_PER_STEP):
            c = c_ref[hh]
            scale_ref[slot, hh] = jnp.exp2(-c)
            c_ref[hh] = c + rs_old_ref[hh]
            rs_old_ref[hh] = rs_new_ref[hh]

    @pl.when(i == 0)
    def _():
        for hh in range(HEADS_PER_STEP):
            vh_ref[hh] = jnp.where(head_masks[hh], v_ref[...], jnp.zeros((), BF16))

    acc_ref[...] = jnp.zeros_like(acc_ref)
    c_ref[...] = jnp.zeros_like(c_ref)
    rs_old_ref[...] = jnp.zeros_like(rs_old_ref)
    rs_new_ref[:, :kb] = jnp.zeros((HEADS_PER_STEP, kb, LANES), F32)
    w_ref[0, :, :kb] = jnp.zeros((HEADS_PER_STEP, kb, kb), BF16)
    scores(0, 0, late_rows, strict[:kb])
    advance_carry(1)
    scores(1, 1, all_rows, strict)
    weights(0, late_rows)

    def body(jj, carry):
        t = 2 + 2 * jj
        for slot in range(2):
            advance_carry(slot)
            scores(t + slot, slot, all_rows, None)
            weights(1 - slot, all_rows)
            output(t + slot - 2, slot)
        return carry

    lax.fori_loop(0, i, body, 0)
    advance_carry(0)
    weights(1, all_rows)
    output(n_full, 0)
    advance_carry(1)
    output(n_full + 1, 1)
    o_ref[...] = (acc_ref[0] + acc_ref[1]).astype(BF16)


def _attn_call(qkv3, tq):
    b, s, _ = qkv3.shape
    n_pairs = SB_HEADS // HEADS_PER_STEP
    return pl.pallas_call(
        functools.partial(_attn_kernel, tq=tq, kb=tq // 2),
        grid=(b, n_pairs, s // tq),
        in_specs=[pl.BlockSpec((None, tq, LANES), lambda bi, hp, i: (bi, i, hp)),
                  pl.BlockSpec((None, s, LANES), lambda bi, hp, i: (bi, 0, n_pairs + hp)),
                  pl.BlockSpec((None, s, LANES), lambda bi, hp, i: (bi, 0, 2 * n_pairs + hp))],
        out_specs=pl.BlockSpec((None, tq, LANES), lambda bi, hp, i: (bi, i, hp)),
        out_shape=jax.ShapeDtypeStruct((b, s, SB_WIDTH), BF16),
        scratch_shapes=[pltpu.VMEM((HEADS_PER_STEP, tq, LANES), F32),
                        pltpu.VMEM((HEADS_PER_STEP, tq, LANES), F32),
                        pltpu.VMEM((2, HEADS_PER_STEP, tq, LANES), F32),
                        pltpu.VMEM((2, HEADS_PER_STEP, tq, tq // 2), BF16),
                        pltpu.VMEM((2, HEADS_PER_STEP, tq, tq // 2), F32),
                        pltpu.VMEM((2, HEADS_PER_STEP, tq, tq // 2), BF16),
                        pltpu.VMEM((HEADS_PER_STEP, tq, LANES), F32),
                        pltpu.VMEM((HEADS_PER_STEP, tq, LANES), F32),
                        pltpu.VMEM((HEADS_PER_STEP, s, LANES), BF16)],
        compiler_params=pltpu.CompilerParams(
            dimension_semantics=("arbitrary", "arbitrary", "arbitrary"),
            vmem_limit_bytes=VMEM_LIMIT_BYTES),
        name="stickbreak_attn",
    )(qkv3, qkv3, qkv3)


def _mixer_kernel(x_ref, a_ref, gpre_ref, wr_ref, lng_ref, lnb_ref, sgw_ref, sgb_ref, pw_ref, ps_ref,
                  wb_ref, wo_ref, gpost_ref, o_ref, tail_ref, bsg_ref, *, tm, tiles_per_seq):
    t = pl.program_id(0)
    seq_tile = t % tiles_per_seq
    x = x_ref[...]
    h = _rmsnorm(x, gpre_ref[...]).astype(BF16)

    o_u, o_v, o_p, o_g = 0, SG_WIDTH, 2 * SG_WIDTH, 2 * SG_WIDTH + POOL_WIDTH
    u = _gelu_tanh(_dot(h, wr_ref[:, o_u:o_u + SG_WIDTH]))
    v = _gelu_tanh(_dot(h, wr_ref[:, o_v:o_v + SG_WIDTH]))
    mu = jnp.mean(v, axis=-1, keepdims=True)
    vc = v - mu
    var = jnp.mean(vc * vc, axis=-1, keepdims=True)
    vn = (vc * lax.rsqrt(var + EPS) * lng_ref[...] + lnb_ref[...]).astype(BF16)
    r_i = lax.broadcasted_iota(jnp.int32, (SG_CHUNK, SG_CHUNK), 0)
    c_i = lax.broadcasted_iota(jnp.int32, (SG_CHUNK, SG_CHUNK), 1)
    for g in range(SG_GROUPS):
        wg = jnp.where(c_i <= r_i, sgw_ref[g], 0.0).astype(BF16)
        lo = g * SG_GROUP_DIM
        for n in range(tm // SG_CHUNK):
            rows = slice(n * SG_CHUNK, (n + 1) * SG_CHUNK)
            mixed = _dot(wg, vn[rows, lo:lo + SG_GROUP_DIM]) + sgb_ref[g]
            bsg_ref[rows, lo:lo + SG_GROUP_DIM] = (u[rows, lo:lo + SG_GROUP_DIM] * mixed).astype(BF16)

    xp = _dot(h, wr_ref[:, o_p:o_p + POOL_WIDTH])

    @pl.when(seq_tile == 0)
    def _():
        tail_ref[...] = jnp.zeros_like(tail_ref)

    ext = jnp.concatenate([tail_ref[...], xp], axis=0)
    tail_ref[...] = xp[tm - POOL_TAIL:, :]
    pos = seq_tile * tm + lax.broadcasted_iota(jnp.int32, (tm, POOL_GROUP_DIM), 0)
    c_parts = []
    for g, win in enumerate(POOL_WINDOWS):
        lo = g * POOL_GROUP_DIM
        s = ext[:, lo:lo + POOL_GROUP_DIM]
        span = 1
        while span < win:
            s = s + pltpu.roll(s, span, 0)
            span *= 2
        count = jnp.minimum(pos + 1, win).astype(F32)
        pooled = s[POOL_TAIL:, :] / count - xp[:, lo:lo + POOL_GROUP_DIM]
        c_parts.append(_dot(pooled.astype(BF16), pw_ref[g]))
    cpool = (jnp.concatenate(c_parts, axis=1) * ps_ref[...]).astype(BF16)

    merged = _sigmoid(_dot(h, wr_ref[:, o_g:o_g + D_MODEL])) * _dot(a_ref[...], wb_ref[0])
    merged += _sigmoid(_dot(h, wr_ref[:, o_g + D_MODEL:o_g + 2 * D_MODEL])) * _dot(bsg_ref[...], wb_ref[1])
    merged += _sigmoid(_dot(h, wr_ref[:, o_g + 2 * D_MODEL:o_g + 3 * D_MODEL])) * _dot(cpool, wb_ref[2])
    y = _dot(merged.astype(BF16), wo_ref[...])
    o_ref[...] = x + _rmsnorm(y, gpost_ref[...])


def _mixer_call(x2, a2, gpre, wr, lng, lnb, sgw, sgb, pw, ps, wb, wo, gpost, tm, seq):
    n = x2.shape[0]
    return pl.pallas_call(
        functools.partial(_mixer_kernel, tm=tm, tiles_per_seq=seq // tm),
        grid=(n // tm,),
        in_specs=[pl.BlockSpec((tm, D_MODEL), lambda i: (i, 0)),
                  pl.BlockSpec((tm, SB_WIDTH), lambda i: (i, 0)),
                  _const_spec((1, D_MODEL)),
                  _const_spec((D_MODEL, REST_WIDTH)),
                  _const_spec((1, SG_WIDTH)),
                  _const_spec((1, SG_WIDTH)),
                  _const_spec((SG_GROUPS, SG_CHUNK, SG_CHUNK)),
                  _const_spec((SG_GROUPS, SG_CHUNK, 1)),
                  _const_spec((len(POOL_WINDOWS), POOL_GROUP_DIM, POOL_GROUP_DIM)),
                  _const_spec((1, POOL_WIDTH)),
                  _const_spec((3, SB_WIDTH, D_MODEL)),
                  _const_spec((D_MODEL, D_MODEL)),
                  _const_spec((1, D_MODEL))],
        out_specs=pl.BlockSpec((tm, D_MODEL), lambda i: (i, 0)),
        out_shape=jax.ShapeDtypeStruct((n, D_MODEL), F32),
        scratch_shapes=[pltpu.VMEM((POOL_TAIL, POOL_WIDTH), F32),
                        pltpu.VMEM((tm, SG_WIDTH), BF16)],
        compiler_params=pltpu.CompilerParams(
            dimension_semantics=("arbitrary",), vmem_limit_bytes=VMEM_LIMIT_BYTES),
        name="mixer",
    )(x2, a2, gpre, wr, lng, lnb, sgw, sgb, pw, ps, wb, wo, gpost)


def _ffn_chunks():
    sizes, left = [], D_FF
    while left:
        sizes.append(min(512, left))
        left -= sizes[-1]
    return sizes


def _ffn_kernel(x_ref, gpre_ref, wup_ref, cw_ref, cb_ref, wdn_ref, gpost_ref, o_ref, tail_ref,
                *, tm, tiles_per_seq):
    t = pl.program_id(0)
    x = x_ref[...]
    h = _rmsnorm(x, gpre_ref[...]).astype(BF16)

    @pl.when(t % tiles_per_seq == 0)
    def _():
        tail_ref[...] = jnp.zeros_like(tail_ref)

    def up_proj(lo, width):
        return [_dot(h, wup_ref[:, part + lo:part + lo + width]) for part in (0, D_FF)]

    def conv(up, lo, width):
        cols = slice(lo, lo + width)
        ext = jnp.concatenate([tail_ref[:, cols], up], axis=0)
        tail_ref[:, cols] = up[tm - CONV_TAIL:, :]
        y = cb_ref[:, cols] + cw_ref[2:3, cols] * ext
        y = y + cw_ref[1:2, cols] * pltpu.roll(ext, 1, 0)
        y = y + cw_ref[0:1, cols] * pltpu.roll(ext, 2, 0)
        return y[CONV_TAIL:, :]

    chunks, lo = [], 0
    for width in _ffn_chunks():
        chunks.append((lo, width))
        lo += width
    acc = jnp.zeros((tm, D_MODEL), F32)
    ups = up_proj(*chunks[0])
    for n, (lo, width) in enumerate(chunks):
        nxt = up_proj(*chunks[n + 1]) if n + 1 < len(chunks) else None
        f = _gelu_tanh(conv(ups[0], lo, width)) * conv(ups[1], D_FF + lo, width)
        acc = acc + _dot(f.astype(BF16), wdn_ref[lo:lo + width, :])
        ups = nxt
    o_ref[...] = x + _rmsnorm(acc, gpost_ref[...])


def _ffn_call(x2, gpre, wup, cw, cb, wdn, gpost, tm, seq):
    n = x2.shape[0]
    return pl.pallas_call(
        functools.partial(_ffn_kernel, tm=tm, tiles_per_seq=seq // tm),
        grid=(n // tm,),
        in_specs=[pl.BlockSpec((tm, D_MODEL), lambda i: (i, 0)),
                  _const_spec((1, D_MODEL)),
                  _const_spec((D_MODEL, 2 * D_FF)),
                  _const_spec((CONV_WIDTH, 2 * D_FF)),
                  _const_spec((1, 2 * D_FF)),
                  _const_spec((D_FF, D_MODEL)),
                  _const_spec((1, D_MODEL))],
        out_specs=pl.BlockSpec((tm, D_MODEL), lambda i: (i, 0)),
        out_shape=jax.ShapeDtypeStruct((n, D_MODEL), F32),
        scratch_shapes=[pltpu.VMEM((CONV_TAIL, 2 * D_FF), F32)],
        compiler_params=pltpu.CompilerParams(
            dimension_semantics=("arbitrary",), vmem_limit_bytes=VMEM_LIMIT_BYTES),
        name="geglu_ffn",
    )(x2, gpre, wup, cw, cb, wdn, gpost)


def _tile(seq, want):
    tile = min(want, seq)
    assert seq % tile == 0 and tile % SG_CHUNK == 0, (seq, tile)
    return tile


def kernel(x, norm_pre_mix, w_in, sg_ln_g, sg_ln_b, sg_w, sg_b, pool_w, pool_scale, w_branch, w_out,
           norm_post_mix, norm_pre_ffn, w_up, conv_w, conv_b, w_down, norm_post_ffn):
    b, s, d = x.shape
    assert d == D_MODEL
    depth = w_in.shape[0]
    tm = _tile(s, 512)
    tq = _tile(s, 512)
    row = lambda p: p.reshape(1, -1)
    x2 = x.reshape(b * s, d)
    for l in range(depth):
        w_qkv = w_in[l, :, :QKV_WIDTH].astype(BF16)
        w_rest = w_in[l, :, QKV_WIDTH:].astype(BF16)
        qkv = _qkv_call(x2, row(norm_pre_mix[l]), w_qkv, tm)
        a = _attn_call(qkv.reshape(b, s, QKV_WIDTH), tq)
        x2 = _mixer_call(x2, a.reshape(b * s, SB_WIDTH), row(norm_pre_mix[l]), w_rest,
                         row(sg_ln_g[l]), row(sg_ln_b[l]), sg_w[l], sg_b[l].reshape(SG_GROUPS, SG_CHUNK, 1),
                         pool_w[l].astype(BF16), row(pool_scale[l]), w_branch[l].astype(BF16),
                         w_out[l].astype(BF16), row(norm_post_mix[l]), tm, s)
        x2 = _ffn_call(x2, row(norm_pre_ffn[l]), w_up[l].astype(BF16), conv_w[l], row(conv_b[l]),
                       w_down[l].astype(BF16), row(norm_post_ffn[l]), tm, s)
    return x2.reshape(b, s, d)
```

```python
import functools
import math

import jax
import jax.numpy as jnp
from jax import lax
from jax.experimental import pallas as pl
from jax.experimental.pallas import tpu as pltpu

D_MODEL = 1024
SB_HEADS = 8
SB_HEAD_DIM = 64
SB_WIDTH = SB_HEADS * SB_HEAD_DIM
SG_GROUPS = 4
SG_GROUP_DIM = 128
SG_WIDTH = SG_GROUPS * SG_GROUP_DIM
SG_CHUNK = 128
POOL_WINDOWS = (2, 4, 8, 16)
POOL_GROUP_DIM = 128
POOL_WIDTH = len(POOL_WINDOWS) * POOL_GROUP_DIM
POOL_TAIL = 16
D_FF = 2816
CONV_WIDTH = 3
CONV_TAIL = 8
EPS = 1e-6
QKV_WIDTH = 3 * SB_WIDTH
REST_WIDTH = 2 * SG_WIDTH + POOL_WIDTH + 3 * D_MODEL

LANES = 128
HEADS_PER_VREG = LANES // SB_HEAD_DIM
HEADS_PER_STEP = HEADS_PER_VREG
STEP_LANES = HEADS_PER_STEP * SB_HEAD_DIM
VMEM_LIMIT_BYTES = 56 * 1024 * 1024

F32 = jnp.float32
BF16 = jnp.bfloat16


def _dot(a, b):
    return jnp.dot(a, b, preferred_element_type=F32)


def _rmsnorm(x, g):
    return x * lax.rsqrt(jnp.mean(x * x, axis=-1, keepdims=True) + EPS) * g


def _gelu_tanh(x):
    c = math.sqrt(2.0 / math.pi)
    return x * (0.5 * (1.0 + jnp.tanh(c * (x + 0.044715 * (x * x * x)))))


def _sigmoid(x):
    return 1.0 / (1.0 + jnp.exp(-x))


def _const_spec(shape):
    return pl.BlockSpec(shape, lambda *_: (0,) * len(shape), pipeline_mode=pl.Buffered(1))


Q_SCALE = math.log2(math.e) * SB_HEAD_DIM ** -0.5
LN2_INV = math.log2(math.e)


def _qkv_kernel(x_ref, g_ref, w_ref, o_ref):
    h = _rmsnorm(x_ref[...], g_ref[...]).astype(BF16)
    o_ref[:, :SB_WIDTH] = (_dot(h, w_ref[:, :SB_WIDTH]) * Q_SCALE).astype(BF16)
    o_ref[:, SB_WIDTH:] = _dot(h, w_ref[:, SB_WIDTH:]).astype(BF16)


def _qkv_call(x2, g, w, tm):
    n = x2.shape[0]
    return pl.pallas_call(
        _qkv_kernel,
        grid=(n // tm,),
        in_specs=[pl.BlockSpec((tm, D_MODEL), lambda i: (i, 0)),
                  _const_spec((1, D_MODEL)),
                  _const_spec((D_MODEL, QKV_WIDTH))],
        out_specs=pl.BlockSpec((tm, QKV_WIDTH), lambda i: (i, 0)),
        out_shape=jax.ShapeDtypeStruct((n, QKV_WIDTH), BF16),
        compiler_params=pltpu.CompilerParams(
            dimension_semantics=("arbitrary",), vmem_limit_bytes=VMEM_LIMIT_BYTES),
        name="qkv_proj",
    )(x2, g, w)


MASKED_LOGIT = -1e30
SOFTPLUS_CLAMP = 100.0


def _attn_kernel(q_ref, k_ref, v_ref, o_ref, acc_ref, c_ref, negc_ref, sp_ref, z_ref, w_ref, rs_ref, vh_ref,
                 *, tq, kb):
    i = pl.program_id(2)
    n_full = 2 * i
    lane = lax.broadcasted_iota(jnp.int32, (1, LANES), 1)
    col_blocks = [slice((hh // HEADS_PER_VREG) * LANES, (hh // HEADS_PER_VREG + 1) * LANES)
                  for hh in range(HEADS_PER_STEP)]
    head_masks = [(lane >= (hh % HEADS_PER_VREG) * SB_HEAD_DIM) & (lane < (hh % HEADS_PER_VREG + 1) * SB_HEAD_DIM)
                  for hh in range(HEADS_PER_STEP)]
    r2 = lax.broadcasted_iota(jnp.int32, (kb, kb), 0)
    c2 = lax.broadcasted_iota(jnp.int32, (kb, kb), 1)
    neg_suffix = jnp.where(r2 >= c2, -1.0, 0.0).astype(BF16)
    strict = (lax.broadcasted_iota(jnp.int32, (tq, kb), 1) < lax.broadcasted_iota(jnp.int32, (tq, kb), 0))
    q_heads = [jnp.where(head_masks[hh], q_ref[:, col_blocks[hh]], jnp.zeros((), BF16))
               for hh in range(HEADS_PER_STEP)]
    all_rows = slice(0, tq)
    late_rows = slice(kb, tq)

    def key_start(t):
        return pl.multiple_of((n_full + 1 - t) * kb, kb)

    def scores(t, slot, rows, mask):
        for hh in range(HEADS_PER_STEP):
            kblk = k_ref[pl.ds(key_start(t), kb), col_blocks[hh]]
            z = lax.dot_general(q_heads[hh][rows], kblk, (((1,), (1,)), ((), ())), preferred_element_type=F32)
            sp = jnp.maximum(z, jnp.log(1.0 + jnp.exp2(jnp.minimum(z, SOFTPLUS_CLAMP))) * LN2_INV)
            if mask is not None:
                sp = jnp.where(mask, sp, 0.0)
                z = jnp.where(mask, z, MASKED_LOGIT)
            sp_ref[slot, hh, rows] = sp.astype(BF16)
            z_ref[slot, hh, rows] = z
            rs_ref[hh, rows] = jnp.broadcast_to(jnp.sum(sp, axis=-1, keepdims=True), (sp.shape[0], LANES))

    def weights(slot, rows):
        for hh in range(HEADS_PER_STEP):
            suffix = _dot(sp_ref[slot, hh, rows], neg_suffix)
            neg_c = negc_ref[slot, hh, rows]
            exponent = z_ref[slot, hh, rows] + suffix + jnp.concatenate([neg_c] * (kb // LANES), axis=1)
            w_ref[slot, hh, rows] = jnp.exp2(exponent).astype(BF16)

    def output(t, slot):
        for hh in range(HEADS_PER_STEP):
            acc_ref[hh] += _dot(w_ref[slot, hh], vh_ref[hh, pl.ds(key_start(t), kb), :])

    def advance_carry(slot):
        for hh in range(HEADS_PER_STEP):
            neg_c = c_ref[hh] - rs_ref[hh]
            c_ref[hh] = neg_c
            negc_ref[slot, hh] = neg_c

    @pl.when(i == 0)
    def _():
        for hh in range(HEADS_PER_STEP):
            vh_ref[hh] = jnp.where(head_masks[hh], v_ref[:, col_blocks[hh]], jnp.zeros((), BF16))

    acc_ref[...] = jnp.zeros_like(acc_ref)
    c_ref[...] = jnp.zeros_like(c_ref)
    rs_ref[...] = jnp.zeros_like(rs_ref)
    w_ref[0, :, :kb] = jnp.zeros((HEADS_PER_STEP, kb, kb), BF16)
    advance_carry(0)
    scores(0, 0, late_rows, strict[:kb])
    advance_carry(1)
    scores(1, 1, all_rows, strict)
    weights(0, late_rows)

    def body(jj, carry):
        t = 2 + 2 * jj
        for slot in range(2):
            advance_carry(slot)
            scores(t + slot, slot, all_rows, None)
            weights(1 - slot, all_rows)
            output(t + slot - 2, slot)
        return carry

    lax.fori_loop(0, i, body, 0)
    weights(1, all_rows)
    output(n_full, 0)
    output(n_full + 1, 1)
    for hh in range(0, HEADS_PER_STEP, HEADS_PER_VREG):
        o_ref[:, col_blocks[hh]] = sum(acc_ref[hh + j] for j in range(HEADS_PER_VREG)).astype(BF16)


def _attn_call(qkv3, tq):
    b, s, _ = qkv3.shape
    n_groups = SB_HEADS // HEADS_PER_STEP
    return pl.pallas_call(
        functools.partial(_attn_kernel, tq=tq, kb=tq // 2),
        grid=(b, n_groups, s // tq),
        in_specs=[pl.BlockSpec((None, tq, STEP_LANES), lambda bi, g, i: (bi, i, g)),
                  pl.BlockSpec((None, s, STEP_LANES), lambda bi, g, i: (bi, 0, n_groups + g)),
                  pl.BlockSpec((None, s, STEP_LANES), lambda bi, g, i: (bi, 0, 2 * n_groups + g))],
        out_specs=pl.BlockSpec((None, tq, STEP_LANES), lambda bi, g, i: (bi, i, g)),
        out_shape=jax.ShapeDtypeStruct((b, s, SB_WIDTH), BF16),
        scratch_shapes=[pltpu.VMEM((HEADS_PER_STEP, tq, LANES), F32),
                        pltpu.VMEM((HEADS_PER_STEP, tq, LANES), F32),
                        pltpu.VMEM((2, HEADS_PER_STEP, tq, LANES), F32),
                        pltpu.VMEM((2, HEADS_PER_STEP, tq, tq // 2), BF16),
                        pltpu.VMEM((2, HEADS_PER_STEP, tq, tq // 2), F32),
                        pltpu.VMEM((2, HEADS_PER_STEP, tq, tq // 2), BF16),
                        pltpu.VMEM((HEADS_PER_STEP, tq, LANES), F32),
                        pltpu.VMEM((HEADS_PER_STEP, s, LANES), BF16)],
        compiler_params=pltpu.CompilerParams(
            dimension_semantics=("arbitrary", "arbitrary", "arbitrary"),
            vmem_limit_bytes=VMEM_LIMIT_BYTES),
        name="stickbreak_attn",
    )(qkv3, qkv3, qkv3)


def _mixer_kernel(x_ref, a_ref, gpre_ref, wr_ref, lng_ref, lnb_ref, sgw_ref, sgb_ref, pw_ref, ps_ref,
                  wb_ref, wo_ref, gpost_ref, o_ref, tail_ref, bsg_ref, *, tm, tiles_per_seq):
    t = pl.program_id(0)
    seq_tile = t % tiles_per_seq
    x = x_ref[...]
    h = _rmsnorm(x, gpre_ref[...]).astype(BF16)

    o_u, o_v, o_p, o_g = 0, SG_WIDTH, 2 * SG_WIDTH, 2 * SG_WIDTH + POOL_WIDTH
    u_lin = _dot(h, wr_ref[:, o_u:o_u + SG_WIDTH])
    v_lin = _dot(h, wr_ref[:, o_v:o_v + SG_WIDTH])
    xp = _dot(h, wr_ref[:, o_p:o_p + POOL_WIDTH])
    gate_a = _dot(h, wr_ref[:, o_g:o_g + D_MODEL])
    y_a = _dot(a_ref[...], wb_ref[0])

    u = _gelu_tanh(u_lin)
    v = _gelu_tanh(v_lin)
    mu = jnp.mean(v, axis=-1, keepdims=True)
    vc = v - mu
    var = jnp.mean(vc * vc, axis=-1, keepdims=True)
    vn = (vc * lax.rsqrt(var + EPS) * lng_ref[...] + lnb_ref[...]).astype(BF16)
    r_i = lax.broadcasted_iota(jnp.int32, (SG_CHUNK, SG_CHUNK), 0)
    c_i = lax.broadcasted_iota(jnp.int32, (SG_CHUNK, SG_CHUNK), 1)
    for g in range(SG_GROUPS):
        wg = jnp.where(c_i <= r_i, sgw_ref[g], 0.0).astype(BF16)
        lo = g * SG_GROUP_DIM
        for n in range(tm // SG_CHUNK):
            rows = slice(n * SG_CHUNK, (n + 1) * SG_CHUNK)
            mixed = _dot(wg, vn[rows, lo:lo + SG_GROUP_DIM]) + sgb_ref[g]
            bsg_ref[rows, lo:lo + SG_GROUP_DIM] = (u[rows, lo:lo + SG_GROUP_DIM] * mixed).astype(BF16)
    gate_b = _dot(h, wr_ref[:, o_g + D_MODEL:o_g + 2 * D_MODEL])
    merged = _sigmoid(gate_a) * y_a

    @pl.when(seq_tile == 0)
    def _():
        tail_ref[...] = jnp.zeros_like(tail_ref)

    ext = jnp.concatenate([tail_ref[...], xp], axis=0)
    tail_ref[...] = xp[tm - POOL_TAIL:, :]
    pos = seq_tile * tm + lax.broadcasted_iota(jnp.int32, (tm, POOL_GROUP_DIM), 0)
    c_parts = []
    for g, win in enumerate(POOL_WINDOWS):
        lo = g * POOL_GROUP_DIM
        s = ext[:, lo:lo + POOL_GROUP_DIM]
        span = 1
        while span < win:
            s = s + pltpu.roll(s, span, 0)
            span *= 2
        count = jnp.minimum(pos + 1, win).astype(F32)
        pooled = s[POOL_TAIL:, :] / count - xp[:, lo:lo + POOL_GROUP_DIM]
        c_parts.append(_dot(pooled.astype(BF16), pw_ref[g]))
    y_b = _dot(bsg_ref[...], wb_ref[1])
    gate_c = _dot(h, wr_ref[:, o_g + 2 * D_MODEL:o_g + 3 * D_MODEL])
    cpool = (jnp.concatenate(c_parts, axis=1) * ps_ref[...]).astype(BF16)
    merged += _sigmoid(gate_b) * y_b

    merged += _sigmoid(gate_c) * _dot(cpool, wb_ref[2])
    y = _dot(merged.astype(BF16), wo_ref[...])
    o_ref[...] = x + _rmsnorm(y, gpost_ref[...])


def _mixer_call(x2, a2, gpre, wr, lng, lnb, sgw, sgb, pw, ps, wb, wo, gpost, tm, seq):
    n = x2.shape[0]
    return pl.pallas_call(
        functools.partial(_mixer_kernel, tm=tm, tiles_per_seq=seq // tm),
        grid=(n // tm,),
        in_specs=[pl.BlockSpec((tm, D_MODEL), lambda i: (i, 0)),
                  pl.BlockSpec((tm, SB_WIDTH), lambda i: (i, 0)),
                  _const_spec((1, D_MODEL)),
                  _const_spec((D_MODEL, REST_WIDTH)),
                  _const_spec((1, SG_WIDTH)),
                  _const_spec((1, SG_WIDTH)),
                  _const_spec((SG_GROUPS, SG_CHUNK, SG_CHUNK)),
                  _const_spec((SG_GROUPS, SG_CHUNK, 1)),
                  _const_spec((len(POOL_WINDOWS), POOL_GROUP_DIM, POOL_GROUP_DIM)),
                  _const_spec((1, POOL_WIDTH)),
                  _const_spec((3, SB_WIDTH, D_MODEL)),
                  _const_spec((D_MODEL, D_MODEL)),
                  _const_spec((1, D_MODEL))],
        out_specs=pl.BlockSpec((tm, D_MODEL), lambda i: (i, 0)),
        out_shape=jax.ShapeDtypeStruct((n, D_MODEL), F32),
        scratch_shapes=[pltpu.VMEM((POOL_TAIL, POOL_WIDTH), F32),
                        pltpu.VMEM((tm, SG_WIDTH), BF16)],
        compiler_params=pltpu.CompilerParams(
            dimension_semantics=("arbitrary",), vmem_limit_bytes=VMEM_LIMIT_BYTES),
        name="mixer",
    )(x2, a2, gpre, wr, lng, lnb, sgw, sgb, pw, ps, wb, wo, gpost)


def _ffn_chunks():
    sizes, left = [], D_FF
    while left:
        sizes.append(min(512, left))
        left -= sizes[-1]
    return sizes


def _ffn_kernel(x_ref, gpre_ref, wup_ref, cw_ref, cb_ref, wdn_ref, gpost_ref, o_ref, tail_ref,
                *, tm, tiles_per_seq):
    t = pl.program_id(0)
    x = x_ref[...]
    h = _rmsnorm(x, gpre_ref[...]).astype(BF16)

    @pl.when(t % tiles_per_seq == 0)
    def _():
        tail_ref[...] = jnp.zeros_like(tail_ref)

    def up_proj(lo, width):
        return [_dot(h, wup_ref[:, part + lo:part + lo + width]) for part in (0, D_FF)]

    def conv(up, lo, width):
        cols = slice(lo, lo + width)
        ext = jnp.concatenate([tail_ref[:, cols], up], axis=0)
        tail_ref[:, cols] = up[tm - CONV_TAIL:, :]
        y = cb_ref[:, cols] + cw_ref[2:3, cols] * ext
        y = y + cw_ref[1:2, cols] * pltpu.roll(ext, 1, 0)
        y = y + cw_ref[0:1, cols] * pltpu.roll(ext, 2, 0)
        return y[CONV_TAIL:, :]

    chunks, lo = [], 0
    for width in _ffn_chunks():
        chunks.append((lo, width))
        lo += width
    acc = jnp.zeros((tm, D_MODEL), F32)
    ups = up_proj(*chunks[0])
    for n, (lo, width) in enumerate(chunks):
        nxt = up_proj(*chunks[n + 1]) if n + 1 < len(chunks) else None
        f = _gelu_tanh(conv(ups[0], lo, width)) * conv(ups[1], D_FF + lo, width)
        acc = acc + _dot(f.astype(BF16), wdn_ref[lo:lo + width, :])
        ups = nxt
    o_ref[...] = x + _rmsnorm(acc, gpost_ref[...])


def _ffn_call(x2, gpre, wup, cw, cb, wdn, gpost, tm, seq):
    n = x2.shape[0]
    return pl.pallas_call(
        functools.partial(_ffn_kernel, tm=tm, tiles_per_seq=seq // tm),
        grid=(n // tm,),
        in_specs=[pl.BlockSpec((tm, D_MODEL), lambda i: (i, 0)),
                  _const_spec((1, D_MODEL)),
                  _const_spec((D_MODEL, 2 * D_FF)),
                  _const_spec((CONV_WIDTH, 2 * D_FF)),
                  _const_spec((1, 2 * D_FF)),
                  _const_spec((D_FF, D_MODEL)),
                  _const_spec((1, D_MODEL))],
        out_specs=pl.BlockSpec((tm, D_MODEL), lambda i: (i, 0)),
        out_shape=jax.ShapeDtypeStruct((n, D_MODEL), F32),
        scratch_shapes=[pltpu.VMEM((CONV_TAIL, 2 * D_FF), F32)],
        compiler_params=pltpu.CompilerParams(
            dimension_semantics=("arbitrary",), vmem_limit_bytes=VMEM_LIMIT_BYTES),
        name="geglu_ffn",
    )(x2, gpre, wup, cw, cb, wdn, gpost)


def _tile(seq, want):
    tile = min(want, seq)
    assert seq % tile == 0 and tile % SG_CHUNK == 0, (seq, tile)
    return tile


def kernel(x, norm_pre_mix, w_in, sg_ln_g, sg_ln_b, sg_w, sg_b, pool_w, pool_scale, w_branch, w_out,
           norm_post_mix, norm_pre_ffn, w_up, conv_w, conv_b, w_down, norm_post_ffn):
    b, s, d = x.shape
    assert d == D_MODEL
    depth = w_in.shape[0]
    tm = _tile(s, 512)
    tq = _tile(s, 512)
    row = lambda p: p.reshape(1, -1)
    x2 = x.reshape(b * s, d)
    for l in range(depth):
        w_qkv = w_in[l, :, :QKV_WIDTH].astype(BF16)
        w_rest = w_in[l, :, QKV_WIDTH:].astype(BF16)
        qkv = _qkv_call(x2, row(norm_pre_mix[l]), w_qkv, tm)
        a = _attn_call(qkv.reshape(b, s, QKV_WIDTH), tq)
        x2 = _mixer_call(x2, a.reshape(b * s, SB_WIDTH), row(norm_pre_mix[l]), w_rest,
                         row(sg_ln_g[l]), row(sg_ln_b[l]), sg_w[l], sg_b[l].reshape(SG_GROUPS, SG_CHUNK, 1),
                         pool_w[l].astype(BF16), row(pool_scale[l]), w_branch[l].astype(BF16),
                         w_out[l].astype(BF16), row(norm_post_mix[l]), tm, s)
        x2 = _ffn_call(x2, row(norm_pre_ffn[l]), w_up[l].astype(BF16), conv_w[l], row(conv_b[l]),
                       w_down[l].astype(BF16), row(norm_post_ffn[l]), tm, s)
    return x2.reshape(b, s, d)
```

```python
import functools
import math

import jax
import jax.numpy as jnp
from jax import lax
from jax.experimental import pallas as pl
from jax.experimental.pallas import tpu as pltpu

D_MODEL = 1024
SB_HEADS = 8
SB_HEAD_DIM = 64
SB_WIDTH = SB_HEADS * SB_HEAD_DIM
SG_GROUPS = 4
SG_GROUP_DIM = 128
SG_WIDTH = SG_GROUPS * SG_GROUP_DIM
SG_CHUNK = 128
POOL_WINDOWS = (2, 4, 8, 16)
POOL_GROUP_DIM = 128
POOL_WIDTH = len(POOL_WINDOWS) * POOL_GROUP_DIM
POOL_TAIL = 16
D_FF = 2816
CONV_WIDTH = 3
CONV_TAIL = 8
EPS = 1e-6
QKV_WIDTH = 3 * SB_WIDTH
REST_WIDTH = 2 * SG_WIDTH + POOL_WIDTH + 3 * D_MODEL

LANES = 128
HEADS_PER_VREG = LANES // SB_HEAD_DIM
HEADS_PER_STEP = HEADS_PER_VREG
N_COL_BLOCKS = HEADS_PER_STEP // HEADS_PER_VREG
STEP_LANES = N_COL_BLOCKS * LANES
VMEM_LIMIT_BYTES = 56 * 1024 * 1024

F32 = jnp.float32
BF16 = jnp.bfloat16


def _dot(a, b):
    return jnp.dot(a, b, preferred_element_type=F32)


def _rmsnorm(x, g):
    return x * lax.rsqrt(jnp.mean(x * x, axis=-1, keepdims=True) + EPS) * g


def _gelu_tanh(x):
    c = math.sqrt(2.0 / math.pi)
    return x * (0.5 * (1.0 + jnp.tanh(c * (x + 0.044715 * (x * x * x)))))


def _sigmoid(x):
    return 1.0 / (1.0 + jnp.exp(-x))


def _const_spec(shape):
    return pl.BlockSpec(shape, lambda *_: (0,) * len(shape), pipeline_mode=pl.Buffered(1))


Q_SCALE = math.log2(math.e) * SB_HEAD_DIM ** -0.5
LN2_INV = math.log2(math.e)


def _qkv_kernel(x_ref, g_ref, w_ref, o_ref):
    h = _rmsnorm(x_ref[...], g_ref[...]).astype(BF16)
    o_ref[:, :SB_WIDTH] = (_dot(h, w_ref[:, :SB_WIDTH]) * Q_SCALE).astype(BF16)
    o_ref[:, SB_WIDTH:] = _dot(h, w_ref[:, SB_WIDTH:]).astype(BF16)


def _qkv_call(x2, g, w, tm):
    n = x2.shape[0]
    return pl.pallas_call(
        _qkv_kernel,
        grid=(n // tm,),
        in_specs=[pl.BlockSpec((tm, D_MODEL), lambda i: (i, 0)),
                  _const_spec((1, D_MODEL)),
                  _const_spec((D_MODEL, QKV_WIDTH))],
        out_specs=pl.BlockSpec((tm, QKV_WIDTH), lambda i: (i, 0)),
        out_shape=jax.ShapeDtypeStruct((n, QKV_WIDTH), BF16),
        compiler_params=pltpu.CompilerParams(
            dimension_semantics=("arbitrary",), vmem_limit_bytes=VMEM_LIMIT_BYTES),
        name="qkv_proj",
    )(x2, g, w)


MASKED_LOGIT = -1e30
SOFTPLUS_CLAMP = 100.0


def _attn_kernel(q_ref, k_ref, v_ref, o_ref, acc_ref, c_ref, negc_ref, sp_ref, z_ref, w_ref, rs_ref, vh_ref,
                 *, tq, kb):
    i = pl.program_id(2)
    n_full = 2 * i
    lane = lax.broadcasted_iota(jnp.int32, (1, LANES), 1)
    col_blocks = [slice((hh // HEADS_PER_VREG) * LANES, (hh // HEADS_PER_VREG + 1) * LANES)
                  for hh in range(HEADS_PER_STEP)]
    head_masks = [(lane >= (hh % HEADS_PER_VREG) * SB_HEAD_DIM) & (lane < (hh % HEADS_PER_VREG + 1) * SB_HEAD_DIM)
                  for hh in range(HEADS_PER_STEP)]
    r2 = lax.broadcasted_iota(jnp.int32, (kb, kb), 0)
    c2 = lax.broadcasted_iota(jnp.int32, (kb, kb), 1)
    neg_suffix = jnp.where(r2 >= c2, -1.0, 0.0).astype(BF16)
    strict = (lax.broadcasted_iota(jnp.int32, (tq, kb), 1) < lax.broadcasted_iota(jnp.int32, (tq, kb), 0))
    q_heads = [jnp.where(head_masks[hh], q_ref[:, col_blocks[hh]], jnp.zeros((), BF16))
               for hh in range(HEADS_PER_STEP)]
    all_rows = slice(0, tq)
    late_rows = slice(kb, tq)

    def key_start(t):
        return pl.multiple_of((n_full + 1 - t) * kb, kb)

    def scores(t, slot, rows, mask):
        for hh in range(HEADS_PER_STEP):
            kblk = k_ref[pl.ds(key_start(t), kb), col_blocks[hh]]
            z = lax.dot_general(q_heads[hh][rows], kblk, (((1,), (1,)), ((), ())), preferred_element_type=F32)
            sp = jnp.maximum(z, jnp.log(1.0 + jnp.exp2(jnp.minimum(z, SOFTPLUS_CLAMP))) * LN2_INV)
            if mask is not None:
                sp = jnp.where(mask, sp, 0.0)
                z = jnp.where(mask, z, MASKED_LOGIT)
            sp_ref[slot, hh, rows] = sp.astype(BF16)
            z_ref[slot, hh, rows] = z
            rs_ref[hh, rows] = jnp.broadcast_to(jnp.sum(sp, axis=-1, keepdims=True), (sp.shape[0], LANES))

    def weights(slot, rows):
        for hh in range(HEADS_PER_STEP):
            suffix = _dot(sp_ref[slot, hh, rows], neg_suffix)
            neg_c = negc_ref[slot, hh, rows]
            exponent = z_ref[slot, hh, rows] + suffix + jnp.concatenate([neg_c] * (kb // LANES), axis=1)
            blk, sub = divmod(hh, HEADS_PER_VREG)
            w_ref[slot, blk, rows, sub * kb:(sub + 1) * kb] = jnp.exp2(exponent).astype(BF16)

    def output(t, slot):
        for blk in range(N_COL_BLOCKS):
            acc_ref[blk] += _dot(w_ref[slot, blk], vh_ref[blk, n_full + 1 - t])

    def advance_carry(slot):
        for hh in range(HEADS_PER_STEP):
            neg_c = c_ref[hh] - rs_ref[hh]
            c_ref[hh] = neg_c
            negc_ref[slot, hh] = neg_c

    @pl.when(i == 0)
    def _():
        for j in range(v_ref.shape[0] // kb):
            for hh in range(HEADS_PER_STEP):
                blk, sub = divmod(hh, HEADS_PER_VREG)
                vh_ref[blk, j, sub * kb:(sub + 1) * kb] = jnp.where(
                    head_masks[hh], v_ref[j * kb:(j + 1) * kb, col_blocks[hh]], jnp.zeros((), BF16))

    acc_ref[...] = jnp.zeros_like(acc_ref)
    c_ref[...] = jnp.zeros_like(c_ref)
    rs_ref[...] = jnp.zeros_like(rs_ref)
    w_ref[0, :, :kb] = jnp.zeros((N_COL_BLOCKS, kb, HEADS_PER_VREG * kb), BF16)
    advance_carry(0)
    scores(0, 0, late_rows, strict[:kb])
    advance_carry(1)
    scores(1, 1, all_rows, strict)
    weights(0, late_rows)

    def body(jj, carry):
        t = 2 + 2 * jj
        for slot in range(2):
            advance_carry(slot)
            scores(t + slot, slot, all_rows, None)
            weights(1 - slot, all_rows)
            output(t + slot - 2, slot)
        return carry

    lax.fori_loop(0, i, body, 0)
    weights(1, all_rows)
    output(n_full, 0)
    output(n_full + 1, 1)
    for blk in range(N_COL_BLOCKS):
        o_ref[:, blk * LANES:(blk + 1) * LANES] = acc_ref[blk].astype(BF16)


def _attn_call(qkv3, tq):
    b, s, _ = qkv3.shape
    n_groups = SB_HEADS // HEADS_PER_STEP
    return pl.pallas_call(
        functools.partial(_attn_kernel, tq=tq, kb=tq // 2),
        grid=(b, n_groups, s // tq),
        in_specs=[pl.BlockSpec((None, tq, STEP_LANES), lambda bi, g, i: (bi, i, g)),
                  pl.BlockSpec((None, s, STEP_LANES), lambda bi, g, i: (bi, 0, n_groups + g)),
                  pl.BlockSpec((None, s, STEP_LANES), lambda bi, g, i: (bi, 0, 2 * n_groups + g))],
        out_specs=pl.BlockSpec((None, tq, STEP_LANES), lambda bi, g, i: (bi, i, g)),
        out_shape=jax.ShapeDtypeStruct((b, s, SB_WIDTH), BF16),
        scratch_shapes=[pltpu.VMEM((N_COL_BLOCKS, tq, LANES), F32),
                        pltpu.VMEM((HEADS_PER_STEP, tq, LANES), F32),
                        pltpu.VMEM((2, HEADS_PER_STEP, tq, LANES), F32),
                        pltpu.VMEM((2, HEADS_PER_STEP, tq, tq // 2), BF16),
                        pltpu.VMEM((2, HEADS_PER_STEP, tq, tq // 2), F32),
                        pltpu.VMEM((2, N_COL_BLOCKS, tq, HEADS_PER_VREG * (tq // 2)), BF16),
                        pltpu.VMEM((HEADS_PER_STEP, tq, LANES), F32),
                        pltpu.VMEM((N_COL_BLOCKS, s // (tq // 2), HEADS_PER_VREG * (tq // 2), LANES), BF16)],
        compiler_params=pltpu.CompilerParams(
            dimension_semantics=("arbitrary", "arbitrary", "arbitrary"),
            vmem_limit_bytes=VMEM_LIMIT_BYTES),
        name="stickbreak_attn",
    )(qkv3, qkv3, qkv3)


def _mixer_kernel(x_ref, a_ref, gpre_ref, wr_ref, lng_ref, lnb_ref, sgw_ref, sgb_ref, pw_ref, ps_ref,
                  wb_ref, wo_ref, gpost_ref, o_ref, tail_ref, bsg_ref, *, tm, tiles_per_seq):
    t = pl.program_id(0)
    seq_tile = t % tiles_per_seq
    x = x_ref[...]
    h = _rmsnorm(x, gpre_ref[...]).astype(BF16)

    o_u, o_v, o_p, o_g = 0, SG_WIDTH, 2 * SG_WIDTH, 2 * SG_WIDTH + POOL_WIDTH
    u_lin = _dot(h, wr_ref[:, o_u:o_u + SG_WIDTH])
    v_lin = _dot(h, wr_ref[:, o_v:o_v + SG_WIDTH])
    xp = _dot(h, wr_ref[:, o_p:o_p + POOL_WIDTH])
    gate_a = _dot(h, wr_ref[:, o_g:o_g + D_MODEL])
    y_a = _dot(a_ref[...], wb_ref[0])

    u = _gelu_tanh(u_lin)
    v = _gelu_tanh(v_lin)
    mu = jnp.mean(v, axis=-1, keepdims=True)
    vc = v - mu
    var = jnp.mean(vc * vc, axis=-1, keepdims=True)
    vn = (vc * lax.rsqrt(var + EPS) * lng_ref[...] + lnb_ref[...]).astype(BF16)
    r_i = lax.broadcasted_iota(jnp.int32, (SG_CHUNK, SG_CHUNK), 0)
    c_i = lax.broadcasted_iota(jnp.int32, (SG_CHUNK, SG_CHUNK), 1)
    for g in range(SG_GROUPS):
        wg = jnp.where(c_i <= r_i, sgw_ref[g], 0.0).astype(BF16)
        lo = g * SG_GROUP_DIM
        for n in range(tm // SG_CHUNK):
            rows = slice(n * SG_CHUNK, (n + 1) * SG_CHUNK)
            mixed = _dot(wg, vn[rows, lo:lo + SG_GROUP_DIM]) + sgb_ref[g]
            bsg_ref[rows, lo:lo + SG_GROUP_DIM] = (u[rows, lo:lo + SG_GROUP_DIM] * mixed).astype(BF16)
    gate_b = _dot(h, wr_ref[:, o_g + D_MODEL:o_g + 2 * D_MODEL])
    merged = _sigmoid(gate_a) * y_a

    @pl.when(seq_tile == 0)
    def _():
        tail_ref[...] = jnp.zeros_like(tail_ref)

    ext = jnp.concatenate([tail_ref[...], xp], axis=0)
    tail_ref[...] = xp[tm - POOL_TAIL:, :]
    pos = seq_tile * tm + lax.broadcasted_iota(jnp.int32, (tm, POOL_GROUP_DIM), 0)
    c_parts = []
    for g, win in enumerate(POOL_WINDOWS):
        lo = g * POOL_GROUP_DIM
        s = ext[:, lo:lo + POOL_GROUP_DIM]
        span = 1
        while span < win:
            s = s + pltpu.roll(s, span, 0)
            span *= 2
        count = jnp.minimum(pos + 1, win).astype(F32)
        pooled = s[POOL_TAIL:, :] / count - xp[:, lo:lo + POOL_GROUP_DIM]
        c_parts.append(_dot(pooled.astype(BF16), pw_ref[g]))
    y_b = _dot(bsg_ref[...], wb_ref[1])
    gate_c = _dot(h, wr_ref[:, o_g + 2 * D_MODEL:o_g + 3 * D_MODEL])
    cpool = (jnp.concatenate(c_parts, axis=1) * ps_ref[...]).astype(BF16)
    merged += _sigmoid(gate_b) * y_b

    merged += _sigmoid(gate_c) * _dot(cpool, wb_ref[2])
    y = _dot(merged.astype(BF16), wo_ref[...])
    o_ref[...] = x + _rmsnorm(y, gpost_ref[...])


def _mixer_call(x2, a2, gpre, wr, lng, lnb, sgw, sgb, pw, ps, wb, wo, gpost, tm, seq):
    n = x2.shape[0]
    return pl.pallas_call(
        functools.partial(_mixer_kernel, tm=tm, tiles_per_seq=seq // tm),
        grid=(n // tm,),
        in_specs=[pl.BlockSpec((tm, D_MODEL), lambda i: (i, 0)),
                  pl.BlockSpec((tm, SB_WIDTH), lambda i: (i, 0)),
                  _const_spec((1, D_MODEL)),
                  _const_spec((D_MODEL, REST_WIDTH)),
                  _const_spec((1, SG_WIDTH)),
                  _const_spec((1, SG_WIDTH)),
                  _const_spec((SG_GROUPS, SG_CHUNK, SG_CHUNK)),
                  _const_spec((SG_GROUPS, SG_CHUNK, 1)),
                  _const_spec((len(POOL_WINDOWS), POOL_GROUP_DIM, POOL_GROUP_DIM)),
                  _const_spec((1, POOL_WIDTH)),
                  _const_spec((3, SB_WIDTH, D_MODEL)),
                  _const_spec((D_MODEL, D_MODEL)),
                  _const_spec((1, D_MODEL))],
        out_specs=pl.BlockSpec((tm, D_MODEL), lambda i: (i, 0)),
        out_shape=jax.ShapeDtypeStruct((n, D_MODEL), F32),
        scratch_shapes=[pltpu.VMEM((POOL_TAIL, POOL_WIDTH), F32),
                        pltpu.VMEM((tm, SG_WIDTH), BF16)],
        compiler_params=pltpu.CompilerParams(
            dimension_semantics=("arbitrary",), vmem_limit_bytes=VMEM_LIMIT_BYTES),
        name="mixer",
    )(x2, a2, gpre, wr, lng, lnb, sgw, sgb, pw, ps, wb, wo, gpost)


FFN_CHUNK = D_FF


def _ffn_chunks():
    sizes, left = [], D_FF
    while left:
        sizes.append(min(FFN_CHUNK, left))
        left -= sizes[-1]
    return sizes


def _ffn_kernel(x_ref, gpre_ref, wup_ref, cw_ref, cb_ref, wdn_ref, gpost_ref, o_ref, tail_ref,
                *, tm, tiles_per_seq):
    t = pl.program_id(0)
    x = x_ref[...]
    h = _rmsnorm(x, gpre_ref[...]).astype(BF16)

    @pl.when(t % tiles_per_seq == 0)
    def _():
        tail_ref[...] = jnp.zeros_like(tail_ref)

    def up_proj(lo, width):
        return [_dot(h, wup_ref[:, part + lo:part + lo + width]) for part in (0, D_FF)]

    def conv(up, lo, width):
        cols = slice(lo, lo + width)
        ext = jnp.concatenate([tail_ref[:, cols], up], axis=0)
        tail_ref[:, cols] = up[tm - CONV_TAIL:, :]
        y = cb_ref[:, cols] + cw_ref[2:3, cols] * ext
        y = y + cw_ref[1:2, cols] * pltpu.roll(ext, 1, 0)
        y = y + cw_ref[0:1, cols] * pltpu.roll(ext, 2, 0)
        return y[CONV_TAIL:, :]

    chunks, lo = [], 0
    for width in _ffn_chunks():
        chunks.append((lo, width))
        lo += width
    acc = jnp.zeros((tm, D_MODEL), F32)
    ups = up_proj(*chunks[0])
    for n, (lo, width) in enumerate(chunks):
        nxt = up_proj(*chunks[n + 1]) if n + 1 < len(chunks) else None
        f = _gelu_tanh(conv(ups[0], lo, width)) * conv(ups[1], D_FF + lo, width)
        acc = acc + _dot(f.astype(BF16), wdn_ref[lo:lo + width, :])
        ups = nxt
    o_ref[...] = x + _rmsnorm(acc, gpost_ref[...])


def _ffn_call(x2, gpre, wup, cw, cb, wdn, gpost, tm, seq):
    n = x2.shape[0]
    return pl.pallas_call(
        functools.partial(_ffn_kernel, tm=tm, tiles_per_seq=seq // tm),
        grid=(n // tm,),
        in_specs=[pl.BlockSpec((tm, D_MODEL), lambda i: (i, 0)),
                  _const_spec((1, D_MODEL)),
                  _const_spec((D_MODEL, 2 * D_FF)),
                  _const_spec((CONV_WIDTH, 2 * D_FF)),
                  _const_spec((1, 2 * D_FF)),
                  _const_spec((D_FF, D_MODEL)),
                  _const_spec((1, D_MODEL))],
        out_specs=pl.BlockSpec((tm, D_MODEL), lambda i: (i, 0)),
        out_shape=jax.ShapeDtypeStruct((n, D_MODEL), F32),
        scratch_shapes=[pltpu.VMEM((CONV_TAIL, 2 * D_FF), F32)],
        compiler_params=pltpu.CompilerParams(
            dimension_semantics=("arbitrary",), vmem_limit_bytes=VMEM_LIMIT_BYTES),
        name="geglu_ffn",
    )(x2, gpre, wup, cw, cb, wdn, gpost)


def _tile(seq, want):
    tile = min(want, seq)
    assert seq % tile == 0 and tile % SG_CHUNK == 0, (seq, tile)
    return tile


def kernel(x, norm_pre_mix, w_in, sg_ln_g, sg_ln_b, sg_w, sg_b, pool_w, pool_scale, w_branch, w_out,
           norm_post_mix, norm_pre_ffn, w_up, conv_w, conv_b, w_down, norm_post_ffn):
    b, s, d = x.shape
    assert d == D_MODEL
    depth = w_in.shape[0]
    tm = _tile(s, 512)
    tq = _tile(s, 512)
    row = lambda p: p.reshape(1, -1)
    x2 = x.reshape(b * s, d)
    for l in range(depth):
        w_qkv = w_in[l, :, :QKV_WIDTH].astype(BF16)
        w_rest = w_in[l, :, QKV_WIDTH:].astype(BF16)
        qkv = _qkv_call(x2, row(norm_pre_mix[l]), w_qkv, tm)
        a = _attn_call(qkv.reshape(b, s, QKV_WIDTH), tq)
        x2 = _mixer_call(x2, a.reshape(b * s, SB_WIDTH), row(norm_pre_mix[l]), w_rest,
                         row(sg_ln_g[l]), row(sg_ln_b[l]), sg_w[l], sg_b[l].reshape(SG_GROUPS, SG_CHUNK, 1),
                         pool_w[l].astype(BF16), row(pool_scale[l]), w_branch[l].astype(BF16),
                         w_out[l].astype(BF16), row(norm_post_mix[l]), tm, s)
        x2 = _ffn_call(x2, row(norm_pre_ffn[l]), w_up[l].astype(BF16), conv_w[l], row(conv_b[l]),
                       w_down[l].astype(BF16), row(norm_post_ffn[l]), tm, s)
    return x2.reshape(b, s, d)
```

```python
import functools
import math

import jax
import jax.numpy as jnp
from jax import lax
from jax.experimental import pallas as pl
from jax.experimental.pallas import tpu as pltpu

D_MODEL = 1024
SB_HEADS = 8
SB_HEAD_DIM = 64
SB_WIDTH = SB_HEADS * SB_HEAD_DIM
SG_GROUPS = 4
SG_GROUP_DIM = 128
SG_WIDTH = SG_GROUPS * SG_GROUP_DIM
SG_CHUNK = 128
POOL_WINDOWS = (2, 4, 8, 16)
POOL_GROUP_DIM = 128
POOL_WIDTH = len(POOL_WINDOWS) * POOL_GROUP_DIM
POOL_TAIL = 16
D_FF = 2816
CONV_WIDTH = 3
CONV_TAIL = 8
EPS = 1e-6
QKV_WIDTH = 3 * SB_WIDTH
REST_WIDTH = 2 * SG_WIDTH + POOL_WIDTH + 3 * D_MODEL

LANES = 128
HEADS_PER_VREG = LANES // SB_HEAD_DIM
HEADS_PER_STEP = HEADS_PER_VREG
N_COL_BLOCKS = HEADS_PER_STEP // HEADS_PER_VREG
STEP_LANES = N_COL_BLOCKS * LANES
VMEM_LIMIT_BYTES = 56 * 1024 * 1024

F32 = jnp.float32
BF16 = jnp.bfloat16


def _dot(a, b):
    return jnp.dot(a, b, preferred_element_type=F32)


def _rmsnorm(x, g):
    return x * lax.rsqrt(jnp.mean(x * x, axis=-1, keepdims=True) + EPS) * g


def _gelu_tanh(x):
    c = math.sqrt(2.0 / math.pi)
    return x * (0.5 * (1.0 + jnp.tanh(c * (x + 0.044715 * (x * x * x)))))


def _sigmoid(x):
    return 1.0 / (1.0 + jnp.exp(-x))


def _const_spec(shape):
    return pl.BlockSpec(shape, lambda *_: (0,) * len(shape), pipeline_mode=pl.Buffered(1))


Q_SCALE = math.log2(math.e) * SB_HEAD_DIM ** -0.5
LN2_INV = math.log2(math.e)


def _qkv_kernel(x_ref, g_ref, w_ref, o_ref):
    h = _rmsnorm(x_ref[...], g_ref[...]).astype(BF16)
    o_ref[:, :SB_WIDTH] = (_dot(h, w_ref[:, :SB_WIDTH]) * Q_SCALE).astype(BF16)
    o_ref[:, SB_WIDTH:] = _dot(h, w_ref[:, SB_WIDTH:]).astype(BF16)


def _qkv_call(x2, g, w, tm):
    n = x2.shape[0]
    return pl.pallas_call(
        _qkv_kernel,
        grid=(n // tm,),
        in_specs=[pl.BlockSpec((tm, D_MODEL), lambda i: (i, 0)),
                  _const_spec((1, D_MODEL)),
                  _const_spec((D_MODEL, QKV_WIDTH))],
        out_specs=pl.BlockSpec((tm, QKV_WIDTH), lambda i: (i, 0)),
        out_shape=jax.ShapeDtypeStruct((n, QKV_WIDTH), BF16),
        compiler_params=pltpu.CompilerParams(
            dimension_semantics=("arbitrary",), vmem_limit_bytes=VMEM_LIMIT_BYTES),
        name="qkv_proj",
    )(x2, g, w)


ATTN_QUERY_BLOCK = 1024
ATTN_KEY_BLOCK = 256
MASKED_LOGIT = -1e30
SOFTPLUS_CLAMP = 100.0


def _attn_kernel(q_ref, k_ref, v_ref, o_ref, acc_ref, c_ref, negc_ref, sp_ref, z_ref, w_ref, rs_ref, vh_ref,
                 *, tq, kb):
    i = pl.program_id(2)
    n_diag = tq // kb
    n_full = n_diag * i
    lane = lax.broadcasted_iota(jnp.int32, (1, LANES), 1)
    col_blocks = [slice((hh // HEADS_PER_VREG) * LANES, (hh // HEADS_PER_VREG + 1) * LANES)
                  for hh in range(HEADS_PER_STEP)]
    head_masks = [(lane >= (hh % HEADS_PER_VREG) * SB_HEAD_DIM) & (lane < (hh % HEADS_PER_VREG + 1) * SB_HEAD_DIM)
                  for hh in range(HEADS_PER_STEP)]
    r2 = lax.broadcasted_iota(jnp.int32, (kb, kb), 0)
    c2 = lax.broadcasted_iota(jnp.int32, (kb, kb), 1)
    neg_suffix = jnp.where(r2 >= c2, -1.0, 0.0).astype(BF16)
    strict = (lax.broadcasted_iota(jnp.int32, (tq, kb), 1) < lax.broadcasted_iota(jnp.int32, (tq, kb), 0))
    q_heads = [jnp.where(head_masks[hh], q_ref[:, col_blocks[hh]], jnp.zeros((), BF16))
               for hh in range(HEADS_PER_STEP)]
    all_rows = slice(0, tq)

    def key_block(t):
        return n_full + n_diag - 1 - t

    def key_start(t):
        return pl.multiple_of(key_block(t) * kb, kb)

    def scores(t, slot, rows, mask):
        for hh in range(HEADS_PER_STEP):
            kblk = k_ref[pl.ds(key_start(t), kb), col_blocks[hh]]
            z = lax.dot_general(q_heads[hh][rows], kblk, (((1,), (1,)), ((), ())), preferred_element_type=F32)
            sp = jnp.maximum(z, jnp.log(1.0 + jnp.exp2(jnp.minimum(z, SOFTPLUS_CLAMP))) * LN2_INV)
            if mask is not None:
                sp = jnp.where(mask, sp, 0.0)
                z = jnp.where(mask, z, MASKED_LOGIT)
            sp_ref[slot, hh, rows] = sp.astype(BF16)
            z_ref[slot, hh, rows] = z
            rs_ref[hh, rows] = jnp.broadcast_to(jnp.sum(sp, axis=-1, keepdims=True), (sp.shape[0], LANES))

    def weights(slot, rows):
        for hh in range(HEADS_PER_STEP):
            suffix = _dot(sp_ref[slot, hh, rows], neg_suffix)
            neg_c = negc_ref[slot, hh, rows]
            exponent = z_ref[slot, hh, rows] + suffix + jnp.concatenate([neg_c] * (kb // LANES), axis=1)
            blk, sub = divmod(hh, HEADS_PER_VREG)
            w_ref[slot, blk, rows, sub * kb:(sub + 1) * kb] = jnp.exp2(exponent).astype(BF16)

    def output(t, slot):
        for blk in range(N_COL_BLOCKS):
            acc_ref[blk] += _dot(w_ref[slot, blk], vh_ref[blk, key_block(t)])

    def advance_carry(slot):
        for hh in range(HEADS_PER_STEP):
            neg_c = c_ref[hh] - rs_ref[hh]
            c_ref[hh] = neg_c
            negc_ref[slot, hh] = neg_c

    @pl.when(i == 0)
    def _():
        for j in range(v_ref.shape[0] // kb):
            for hh in range(HEADS_PER_STEP):
                blk, sub = divmod(hh, HEADS_PER_VREG)
                vh_ref[blk, j, sub * kb:(sub + 1) * kb] = jnp.where(
                    head_masks[hh], v_ref[j * kb:(j + 1) * kb, col_blocks[hh]], jnp.zeros((), BF16))

    assert n_diag % 2 == 0
    diag_rows = [slice((n_diag - 1 - m) * kb, tq) for m in range(n_diag)]
    acc_ref[...] = jnp.zeros_like(acc_ref)
    c_ref[...] = jnp.zeros_like(c_ref)
    rs_ref[...] = jnp.zeros_like(rs_ref)
    for slot in range(2):
        w_ref[slot, :, :diag_rows[slot].start] = jnp.zeros(
            (N_COL_BLOCKS, diag_rows[slot].start, HEADS_PER_VREG * kb), BF16)
    for m in range(n_diag):
        advance_carry(m % 2)
        scores(m, m % 2, diag_rows[m], strict[:tq - diag_rows[m].start])
        if m >= 1:
            weights((m - 1) % 2, diag_rows[m - 1])
        if m >= 2:
            output(m - 2, m % 2)

    def body(jj, carry):
        t = n_diag + 2 * jj
        for slot in range(2):
            advance_carry(slot)
            scores(t + slot, slot, all_rows, None)
            weights(1 - slot, all_rows)
            output(t + slot - 2, slot)
        return carry

    lax.fori_loop(0, n_full // 2, body, 0)
    weights(1, all_rows)
    output(n_full + n_diag - 2, 0)
    output(n_full + n_diag - 1, 1)
    for blk in range(N_COL_BLOCKS):
        o_ref[:, blk * LANES:(blk + 1) * LANES] = acc_ref[blk].astype(BF16)


def _attn_call(qkv3, tq, kb):
    b, s, _ = qkv3.shape
    n_groups = SB_HEADS // HEADS_PER_STEP
    return pl.pallas_call(
        functools.partial(_attn_kernel, tq=tq, kb=kb),
        grid=(b, n_groups, s // tq),
        in_specs=[pl.BlockSpec((None, tq, STEP_LANES), lambda bi, g, i: (bi, i, g)),
                  pl.BlockSpec((None, s, STEP_LANES), lambda bi, g, i: (bi, 0, n_groups + g)),
                  pl.BlockSpec((None, s, STEP_LANES), lambda bi, g, i: (bi, 0, 2 * n_groups + g))],
        out_specs=pl.BlockSpec((None, tq, STEP_LANES), lambda bi, g, i: (bi, i, g)),
        out_shape=jax.ShapeDtypeStruct((b, s, SB_WIDTH), BF16),
        scratch_shapes=[pltpu.VMEM((N_COL_BLOCKS, tq, LANES), F32),
                        pltpu.VMEM((HEADS_PER_STEP, tq, LANES), F32),
                        pltpu.VMEM((2, HEADS_PER_STEP, tq, LANES), F32),
                        pltpu.VMEM((2, HEADS_PER_STEP, tq, kb), BF16),
                        pltpu.VMEM((2, HEADS_PER_STEP, tq, kb), F32),
                        pltpu.VMEM((2, N_COL_BLOCKS, tq, HEADS_PER_VREG * kb), BF16),
                        pltpu.VMEM((HEADS_PER_STEP, tq, LANES), F32),
                        pltpu.VMEM((N_COL_BLOCKS, s // kb, HEADS_PER_VREG * kb, LANES), BF16)],
        compiler_params=pltpu.CompilerParams(
            dimension_semantics=("arbitrary", "arbitrary", "arbitrary"),
            vmem_limit_bytes=VMEM_LIMIT_BYTES),
        name="stickbreak_attn",
    )(qkv3, qkv3, qkv3)


def _mixer_kernel(x_ref, a_ref, gpre_ref, wr_ref, lng_ref, lnb_ref, sgw_ref, sgb_ref, pw_ref, ps_ref,
                  wb_ref, wo_ref, gpost_ref, o_ref, tail_ref, bsg_ref, *, tm, tiles_per_seq):
    t = pl.program_id(0)
    seq_tile = t % tiles_per_seq
    x = x_ref[...]
    h = _rmsnorm(x, gpre_ref[...]).astype(BF16)

    o_u, o_v, o_p, o_g = 0, SG_WIDTH, 2 * SG_WIDTH, 2 * SG_WIDTH + POOL_WIDTH
    proj = _dot(h, wr_ref[...])
    u_lin = proj[:, o_u:o_u + SG_WIDTH]
    v_lin = proj[:, o_v:o_v + SG_WIDTH]
    xp = proj[:, o_p:o_p + POOL_WIDTH]
    gate_a = proj[:, o_g:o_g + D_MODEL]
    y_a = _dot(a_ref[...], wb_ref[0])

    u = _gelu_tanh(u_lin)
    v = _gelu_tanh(v_lin)
    mu = jnp.mean(v, axis=-1, keepdims=True)
    vc = v - mu
    var = jnp.mean(vc * vc, axis=-1, keepdims=True)
    vn = (vc * lax.rsqrt(var + EPS) * lng_ref[...] + lnb_ref[...]).astype(BF16)
    r_i = lax.broadcasted_iota(jnp.int32, (SG_CHUNK, SG_CHUNK), 0)
    c_i = lax.broadcasted_iota(jnp.int32, (SG_CHUNK, SG_CHUNK), 1)
    for g in range(SG_GROUPS):
        wg = jnp.where(c_i <= r_i, sgw_ref[g], 0.0).astype(BF16)
        lo = g * SG_GROUP_DIM
        for n in range(tm // SG_CHUNK):
            rows = slice(n * SG_CHUNK, (n + 1) * SG_CHUNK)
            mixed = _dot(wg, vn[rows, lo:lo + SG_GROUP_DIM]) + sgb_ref[g]
            bsg_ref[rows, lo:lo + SG_GROUP_DIM] = (u[rows, lo:lo + SG_GROUP_DIM] * mixed).astype(BF16)
    gate_b = proj[:, o_g + D_MODEL:o_g + 2 * D_MODEL]
    merged = _sigmoid(gate_a) * y_a

    @pl.when(seq_tile == 0)
    def _():
        tail_ref[...] = jnp.zeros_like(tail_ref)

    ext = jnp.concatenate([tail_ref[...], xp], axis=0)
    tail_ref[...] = xp[tm - POOL_TAIL:, :]
    pos = seq_tile * tm + lax.broadcasted_iota(jnp.int32, (tm, POOL_GROUP_DIM), 0)
    c_parts = []
    for g, win in enumerate(POOL_WINDOWS):
        lo = g * POOL_GROUP_DIM
        s = ext[:, lo:lo + POOL_GROUP_DIM]
        span = 1
        while span < win:
            s = s + pltpu.roll(s, span, 0)
            span *= 2
        count = jnp.minimum(pos + 1, win).astype(F32)
        pooled = s[POOL_TAIL:, :] / count - xp[:, lo:lo + POOL_GROUP_DIM]
        c_parts.append(_dot(pooled.astype(BF16), pw_ref[g]))
    y_b = _dot(bsg_ref[...], wb_ref[1])
    gate_c = proj[:, o_g + 2 * D_MODEL:o_g + 3 * D_MODEL]
    cpool = (jnp.concatenate(c_parts, axis=1) * ps_ref[...]).astype(BF16)
    merged += _sigmoid(gate_b) * y_b

    merged += _sigmoid(gate_c) * _dot(cpool, wb_ref[2])
    y = _dot(merged.astype(BF16), wo_ref[...])
    o_ref[...] = x + _rmsnorm(y, gpost_ref[...])


def _mixer_call(x2, a2, gpre, wr, lng, lnb, sgw, sgb, pw, ps, wb, wo, gpost, tm, seq):
    n = x2.shape[0]
    return pl.pallas_call(
        functools.partial(_mixer_kernel, tm=tm, tiles_per_seq=seq // tm),
        grid=(n // tm,),
        in_specs=[pl.BlockSpec((tm, D_MODEL), lambda i: (i, 0)),
                  pl.BlockSpec((tm, SB_WIDTH), lambda i: (i, 0)),
                  _const_spec((1, D_MODEL)),
                  _const_spec((D_MODEL, REST_WIDTH)),
                  _const_spec((1, SG_WIDTH)),
                  _const_spec((1, SG_WIDTH)),
                  _const_spec((SG_GROUPS, SG_CHUNK, SG_CHUNK)),
                  _const_spec((SG_GROUPS, SG_CHUNK, 1)),
                  _const_spec((len(POOL_WINDOWS), POOL_GROUP_DIM, POOL_GROUP_DIM)),
                  _const_spec((1, POOL_WIDTH)),
                  _const_spec((3, SB_WIDTH, D_MODEL)),
                  _const_spec((D_MODEL, D_MODEL)),
                  _const_spec((1, D_MODEL))],
        out_specs=pl.BlockSpec((tm, D_MODEL), lambda i: (i, 0)),
        out_shape=jax.ShapeDtypeStruct((n, D_MODEL), F32),
        scratch_shapes=[pltpu.VMEM((POOL_TAIL, POOL_WIDTH), F32),
                        pltpu.VMEM((tm, SG_WIDTH), BF16)],
        compiler_params=pltpu.CompilerParams(
            dimension_semantics=("arbitrary",), vmem_limit_bytes=VMEM_LIMIT_BYTES),
        name="mixer",
    )(x2, a2, gpre, wr, lng, lnb, sgw, sgb, pw, ps, wb, wo, gpost)


FFN_CHUNK = D_FF


def _ffn_chunks():
    sizes, left = [], D_FF
    while left:
        sizes.append(min(FFN_CHUNK, left))
        left -= sizes[-1]
    return sizes


def _ffn_kernel(x_ref, gpre_ref, wup_ref, cw_ref, cb_ref, wdn_ref, gpost_ref, o_ref, tail_ref,
                *, tm, tiles_per_seq):
    t = pl.program_id(0)
    x = x_ref[...]
    h = _rmsnorm(x, gpre_ref[...]).astype(BF16)

    @pl.when(t % tiles_per_seq == 0)
    def _():
        tail_ref[...] = jnp.zeros_like(tail_ref)

    def up_proj(lo, width):
        return [_dot(h, wup_ref[:, part + lo:part + lo + width]) for part in (0, D_FF)]

    def conv(up, lo, width):
        cols = slice(lo, lo + width)
        ext = jnp.concatenate([tail_ref[:, cols], up], axis=0)
        tail_ref[:, cols] = up[tm - CONV_TAIL:, :]
        y = cb_ref[:, cols] + cw_ref[2:3, cols] * ext
        y = y + cw_ref[1:2, cols] * pltpu.roll(ext, 1, 0)
        y = y + cw_ref[0:1, cols] * pltpu.roll(ext, 2, 0)
        return y[CONV_TAIL:, :]

    chunks, lo = [], 0
    for width in _ffn_chunks():
        chunks.append((lo, width))
        lo += width
    acc = jnp.zeros((tm, D_MODEL), F32)
    ups = up_proj(*chunks[0])
    for n, (lo, width) in enumerate(chunks):
        nxt = up_proj(*chunks[n + 1]) if n + 1 < len(chunks) else None
        f = _gelu_tanh(conv(ups[0], lo, width)) * conv(ups[1], D_FF + lo, width)
        acc = acc + _dot(f.astype(BF16), wdn_ref[lo:lo + width, :])
        ups = nxt
    o_ref[...] = x + _rmsnorm(acc, gpost_ref[...])


def _ffn_call(x2, gpre, wup, cw, cb, wdn, gpost, tm, seq):
    n = x2.shape[0]
    return pl.pallas_call(
        functools.partial(_ffn_kernel, tm=tm, tiles_per_seq=seq // tm),
        grid=(n // tm,),
        in_specs=[pl.BlockSpec((tm, D_MODEL), lambda i: (i, 0)),
                  _const_spec((1, D_MODEL)),
                  _const_spec((D_MODEL, 2 * D_FF)),
                  _const_spec((CONV_WIDTH, 2 * D_FF)),
                  _const_spec((1, 2 * D_FF)),
                  _const_spec((D_FF, D_MODEL)),
                  _const_spec((1, D_MODEL))],
        out_specs=pl.BlockSpec((tm, D_MODEL), lambda i: (i, 0)),
        out_shape=jax.ShapeDtypeStruct((n, D_MODEL), F32),
        scratch_shapes=[pltpu.VMEM((CONV_TAIL, 2 * D_FF), F32)],
        compiler_params=pltpu.CompilerParams(
            dimension_semantics=("arbitrary",), vmem_limit_bytes=VMEM_LIMIT_BYTES),
        name="geglu_ffn",
    )(x2, gpre, wup, cw, cb, wdn, gpost)


def _tile(seq, want):
    tile = min(want, seq)
    assert seq % tile == 0 and tile % SG_CHUNK == 0, (seq, tile)
    return tile


def kernel(x, norm_pre_mix, w_in, sg_ln_g, sg_ln_b, sg_w, sg_b, pool_w, pool_scale, w_branch, w_out,
           norm_post_mix, norm_pre_ffn, w_up, conv_w, conv_b, w_down, norm_post_ffn):
    b, s, d = x.shape
    assert d == D_MODEL
    depth = w_in.shape[0]
    tm = _tile(s, 512)
    tq = _tile(s, ATTN_QUERY_BLOCK)
    kb = min(ATTN_KEY_BLOCK, tq // 2)
    row = lambda p: p.reshape(1, -1)
    x2 = x.reshape(b * s, d)
    for l in range(depth):
        w_qkv = w_in[l, :, :QKV_WIDTH].astype(BF16)
        w_rest = w_in[l, :, QKV_WIDTH:].astype(BF16)
        qkv = _qkv_call(x2, row(norm_pre_mix[l]), w_qkv, tm)
        a = _attn_call(qkv.reshape(b, s, QKV_WIDTH), tq, kb)
        x2 = _mixer_call(x2, a.reshape(b * s, SB_WIDTH), row(norm_pre_mix[l]), w_rest,
                         row(sg_ln_g[l]), row(sg_ln_b[l]), sg_w[l], sg_b[l].reshape(SG_GROUPS, SG_CHUNK, 1),
                         pool_w[l].astype(BF16), row(pool_scale[l]), w_branch[l].astype(BF16),
                         w_out[l].astype(BF16), row(norm_post_mix[l]), tm, s)
        x2 = _ffn_call(x2, row(norm_pre_ffn[l]), w_up[l].astype(BF16), conv_w[l], row(conv_b[l]),
                       w_down[l].astype(BF16), row(norm_post_ffn[l]), tm, s)
    return x2.reshape(b, s, d)
```
